```python
import math
import jax
import jax.numpy as jnp
from jax import lax
import numpy as np

D_MODEL = 2048
BATCH = 8
SEQ = 2048
DEPTH = 4

GRID_W = 64
CTX_LEN = 256
D_MIX = D_MODEL
ATT_HEADS = 8
ATT_HEAD_DIM = 128
ATT_WIDTH = ATT_HEADS * ATT_HEAD_DIM
WIN_R = 8
WIN_C = 16
ROPE_THETA = 10000.0
SSD_WIDTH = D_MIX - ATT_WIDTH
SSD_HEAD_DIM = 64
SSD_HEADS = SSD_WIDTH // SSD_HEAD_DIM
SSD_GROUPS = 2
SSD_STATE = 128
SSD_CONV = 5
SSD_CHUNK = 128
XBC_WIDTH = SSD_WIDTH + 2 * SSD_GROUPS * SSD_STATE
PROJ_WIDTH = 3 * ATT_WIDTH + SSD_WIDTH + XBC_WIDTH + 2 * SSD_HEADS
PEER_HEADS = 8
PEER_NKEYS = 128
PEER_N = PEER_NKEYS * PEER_NKEYS
PEER_QDIM = 256
PEER_TOPK = 16
PEER_BLOCK = 128
DEEPNORM_ALPHA = (2 * DEPTH) ** 0.25
DEEPNORM_BETA = (8 * DEPTH) ** -0.25
N_MOD = 6
EPS = 1e-6

kernel_name = 'hybrid_natten_ssd_peer_diffusion_trunk'


def _layer_norm(x, g, b):
    xf = x.astype(jnp.float32)
    mu = jnp.mean(xf, axis=-1, keepdims=True)
    var = jnp.mean(jnp.square(xf - mu), axis=-1, keepdims=True)
    return ((xf - mu) * lax.rsqrt(var + EPS)).astype(g.dtype) * g + b


def _rms_norm(x, g):
    xf = x.astype(jnp.float32)
    return (xf * lax.rsqrt(jnp.mean(xf * xf, axis=-1, keepdims=True) + EPS)).astype(g.dtype) * g


def _split_proj(p):
    cuts = [ATT_WIDTH, 2 * ATT_WIDTH, 3 * ATT_WIDTH, 3 * ATT_WIDTH + SSD_WIDTH,
            3 * ATT_WIDTH + SSD_WIDTH + XBC_WIDTH]
    return jnp.split(p, cuts, axis=-1)


def _heads(t):
    return t.reshape(t.shape[:2] + (ATT_HEADS, ATT_HEAD_DIM))


def _axial_rope(x):
    L = x.shape[1]
    t = jnp.arange(L)
    pos = jnp.stack([t // GRID_W, t % GRID_W], axis=-1).astype(jnp.float32)
    axis_dim = ATT_HEAD_DIM // 2
    inv_freq = ROPE_THETA ** (-jnp.arange(0, axis_dim, 2, dtype=jnp.float32) / axis_dim)
    ang = pos[:, :, None] * inv_freq
    cos = jnp.cos(ang)[None, :, None]
    sin = jnp.sin(ang)[None, :, None]
    xr = x.reshape(x.shape[:-1] + (2, 2, axis_dim // 2))
    x1, x2 = xr[..., 0, :], xr[..., 1, :]
    out = jnp.stack([x1 * cos - x2 * sin, x1 * sin + x2 * cos], axis=-2)
    return out.reshape(x.shape).astype(x.dtype)


def _neighbourhood_attention(q, k, v, k_ctx, v_ctx, rpb):
    Bt, L, H, dh = q.shape
    rows = L // GRID_W
    wr = min(WIN_R, rows)
    r_idx = jnp.arange(rows)
    c_idx = jnp.arange(GRID_W)
    row_start = jnp.clip(r_idx - WIN_R // 2, 0, rows - wr)
    key_rows = row_start[:, None] + jnp.arange(wr)[None, :]
    col_start = jnp.clip(c_idx - WIN_C // 2, 0, GRID_W - WIN_C)
    col_ok = (c_idx[None, :] >= col_start[:, None]) & (c_idx[None, :] < col_start[:, None] + WIN_C)
    qg = q.reshape(Bt, rows, GRID_W, H, dh)
    kg = k.reshape(Bt, rows, GRID_W, H, dh)[:, key_rows]
    vg = v.reshape(Bt, rows, GRID_W, H, dh)[:, key_rows]
    dr = key_rows - r_idx[:, None] + (WIN_R - 1)
    dc = jnp.clip(c_idx[None, :] - c_idx[:, None] + (WIN_C - 1), 0, 2 * WIN_C - 2)
    bias = rpb[:, dr[:, None, :, None], dc[None, :, None, :]]
    s_nb = jnp.einsum('brqhd,brkwhd->bhrqkw', qg, kg).astype(jnp.float32) + bias
    s_nb = jnp.where(col_ok[:, None, :], s_nb, -jnp.inf)
    s_ctx = jnp.einsum('brqhd,bkhd->bhrqk', qg, k_ctx).astype(jnp.float32)
    n_nb = wr * GRID_W
    s = jnp.concatenate([s_nb.reshape(Bt, H, rows, GRID_W, n_nb), s_ctx], axis=-1)
    p = jax.nn.softmax(s, axis=-1).astype(v.dtype)
    p_nb = p[..., :n_nb].reshape(Bt, H, rows, GRID_W, wr, GRID_W)
    out = (jnp.einsum('bhrqkw,brkwhd->brqhd', p_nb, vg)
           + jnp.einsum('bhrqk,bkhd->brqhd', p[..., n_nb:], v_ctx))
    return out.reshape(Bt, L, H, dh)


def _context_attention(q, k, v):
    s = jnp.einsum('bqhd,bkhd->bhqk', q, k).astype(jnp.float32)
    p = jax.nn.softmax(s, axis=-1).astype(v.dtype)
    return jnp.einsum('bhqk,bkhd->bqhd', p, v)


def _dwconv(x, w, b):
    out = lax.conv_general_dilated(
        x, w[:, None, :], window_strides=(1,),
        padding=[(SSD_CONV // 2, SSD_CONV // 2)],
        dimension_numbers=('NWC', 'WIO', 'NWC'),
        feature_group_count=x.shape[-1])
    return out + b


def _ssd_inputs(xbc, dt_raw, conv_w, conv_b, dt_bias):
    xbc = jax.nn.silu(_dwconv(xbc, conv_w, conv_b))
    xs, bm, cm = jnp.split(xbc, [SSD_WIDTH, SSD_WIDTH + SSD_GROUPS * SSD_STATE], axis=-1)
    Bt, L = xs.shape[:2]
    rep = SSD_HEADS // SSD_GROUPS
    xs = xs.reshape(Bt, L, SSD_HEADS, SSD_HEAD_DIM)
    bm = jnp.repeat(bm.reshape(Bt, L, SSD_GROUPS, SSD_STATE), rep, axis=2)
    cm = jnp.repeat(cm.reshape(Bt, L, SSD_GROUPS, SSD_STATE), rep, axis=2)
    dt = jax.nn.softplus((dt_raw.reshape(Bt, L, 2, SSD_HEADS) + dt_bias).astype(jnp.float32))
    return xs, bm, cm, dt


def _segsum(a):
    T = a.shape[-1]
    a_rep = jnp.broadcast_to(a[..., :, None], a.shape + (T,))
    a_rep = jnp.where(jnp.tril(jnp.ones((T, T), dtype=bool), -1), a_rep, 0.0)
    seg = jnp.cumsum(a_rep, axis=-2)
    return jnp.where(jnp.tril(jnp.ones((T, T), dtype=bool)), seg, -jnp.inf)


def _ssd_scan(x, dt, a, bm, cm, h0, with_y):
    Bt, L, H, P = x.shape
    N = bm.shape[-1]
    nc = L // SSD_CHUNK
    xd = (x * dt[..., None]).reshape(Bt, nc, SSD_CHUNK, H, P)
    bm = bm.reshape(Bt, nc, SSD_CHUNK, H, N)
    cm = cm.reshape(Bt, nc, SSD_CHUNK, H, N)
    a_dt = (dt * a).reshape(Bt, nc, SSD_CHUNK, H).transpose(0, 3, 1, 2)
    a_cs = jnp.cumsum(a_dt, axis=-1)
    decay_states = jnp.exp(a_cs[..., -1:] - a_cs)
    states = jnp.einsum('bclhn,bhcl,bclhp->bchpn', bm, decay_states, xd)
    states = jnp.concatenate([h0[:, None].astype(states.dtype), states], axis=1)
    decay_chunk = jnp.exp(_segsum(jnp.pad(a_cs[..., -1], ((0, 0), (0, 0), (1, 0)))))
    states = jnp.einsum('bhzc,bchpn->bzhpn', decay_chunk, states)
    final = states[:, -1]
    if not with_y:
        return None, final
    l_mat = jnp.exp(_segsum(a_dt))
    y_diag = jnp.einsum('bclhn,bcshn,bhcls,bcshp->bclhp', cm, bm, l_mat, xd)
    y_off = jnp.einsum('bclhn,bchpn,bhcl->bclhp', cm, states[:, :-1], jnp.exp(a_cs))
    return (y_diag + y_off).reshape(Bt, L, H, P).astype(x.dtype), final


def _bidirectional_ssd(ssd_c, ssd_l, a_log, d_skip, with_ctx):
    xs_c, bm_c, cm_c, dt_c = ssd_c
    xs_l, bm_l, cm_l, dt_l = ssd_l
    a = -jnp.exp(a_log.astype(jnp.float32))
    h_zero = jnp.zeros((xs_c.shape[0], SSD_HEADS, SSD_HEAD_DIM, SSD_STATE), jnp.float32)
    ys_c, ys_l = [], []
    for d in range(2):
        f = (lambda t: jnp.flip(t, axis=1)) if d == 1 else (lambda t: t)
        yc, h_ctx = _ssd_scan(f(xs_c), f(dt_c[:, :, d]), a[d], f(bm_c), f(cm_c), h_zero, with_ctx)
        yl, _ = _ssd_scan(f(xs_l), f(dt_l[:, :, d]), a[d], f(bm_l), f(cm_l), h_ctx, True)
        ys_l.append(f(yl) + d_skip[d][:, None] * xs_l)
        if with_ctx:
            ys_c.append(f(yc) + d_skip[d][:, None] * xs_c)
    y_c = ys_c[0] + ys_c[1] if with_ctx else None
    return y_c, ys_l[0] + ys_l[1]


def _mixer(h_c, h_l, w_in, conv_w, conv_b, a_log, dt_bias, d_skip, rpb,
           beta_attn, beta_ssm, w_out, with_ctx):
    q_c, k_c, v_c, z_c, xbc_c, dt_c = _split_proj(h_c @ w_in)
    q_l, k_l, v_l, z_l, xbc_l, dt_l = _split_proj(h_l @ w_in)
    scale = ATT_HEAD_DIM ** -0.5
    k_c, v_c = _heads(k_c), _heads(v_c)
    att_l = _neighbourhood_attention(_axial_rope(_heads(q_l)) * scale, _axial_rope(_heads(k_l)),
                                     _heads(v_l), k_c, v_c, rpb)
    ssd_c = _ssd_inputs(xbc_c, dt_c, conv_w, conv_b, dt_bias)
    ssd_l = _ssd_inputs(xbc_l, dt_l, conv_w, conv_b, dt_bias)
    y_c, y_l = _bidirectional_ssd(ssd_c, ssd_l, a_log, d_skip, with_ctx)

    def merge(att, y, z):
        att = _rms_norm(att.reshape(z.shape[:2] + (ATT_WIDTH,)), beta_attn)
        ssm = _rms_norm(y.reshape(z.shape[:2] + (SSD_WIDTH,)) * jax.nn.silu(z), beta_ssm)
        return jnp.concatenate([att, ssm], axis=-1) @ w_out

    out_l = merge(att_l, y_l, z_l)
    if not with_ctx:
        return None, out_l
    att_c = _context_attention(_heads(q_c) * scale, k_c, v_c)
    return merge(att_c, y_c, z_c), out_l


def _peer(h, wq, sub_keys, u_tab, v_tab):
    T = h.shape[0]
    q = (h @ wq).reshape(T, PEER_HEADS, 2, PEER_QDIM // 2)
    s = jnp.einsum('thad,hakd->thak', q, sub_keys).astype(jnp.float32)
    s_top, i_top = lax.top_k(s, PEER_TOPK)
    kk = PEER_TOPK * PEER_TOPK
    cand_s = (s_top[..., 0, :, None] + s_top[..., 1, None, :]).reshape(T, PEER_HEADS, kk)
    cand_i = (i_top[..., 0, :, None] * PEER_NKEYS + i_top[..., 1, None, :]).reshape(T, PEER_HEADS, kk)
    best_s, best_pos = lax.top_k(cand_s, PEER_TOPK)
    idx = jnp.take_along_axis(cand_i, best_pos, axis=-1)
    gate = jax.nn.softmax(best_s, axis=-1).astype(h.dtype)
    nb = T // PEER_BLOCK

    def block(args):
        hb, ib, gb = args
        act = jax.nn.gelu(jnp.einsum('td,thkd->thk', hb, u_tab[ib]))
        return jnp.einsum('thk,thkd->td', gb * act, v_tab[ib])

    out = lax.map(block, (h.reshape(nb, PEER_BLOCK, D_MODEL),
                          idx.reshape(nb, PEER_BLOCK, PEER_HEADS, PEER_TOPK),
                          gate.reshape(nb, PEER_BLOCK, PEER_HEADS, PEER_TOPK)))
    return out.reshape(T, D_MODEL)


def setup_inputs(seed: int = 0) -> dict:
    key = jax.random.key(seed)
    ks = jax.random.split(key, 24)

    def nrm(k, shape, s):
        return jax.random.normal(k, shape, jnp.float32) * s

    a_log = jnp.log(jax.random.uniform(ks[9], (DEPTH, 2, SSD_HEADS), jnp.float32, 1.0, 16.0))
    dt0 = jnp.exp(jax.random.uniform(ks[10], (DEPTH, 2, SSD_HEADS), jnp.float32,
                                     math.log(1e-3), math.log(1e-1)))
    dt_bias = dt0 + jnp.log(-jnp.expm1(-dt0))
    return {
        'x': nrm(ks[0], (BATCH, SEQ, D_MODEL), 1.0),
        'c': nrm(ks[1], (BATCH, D_MODEL), 1.0),
        'ctx': nrm(ks[2], (BATCH, CTX_LEN, D_MODEL), 1.0),
        'c_ctx': nrm(ks[3], (D_MODEL,), 1.0),
        'ada_w': nrm(ks[4], (DEPTH, D_MODEL, N_MOD * D_MODEL), 0.5 * D_MODEL ** -0.5),
        'ada_b': nrm(ks[5], (DEPTH, N_MOD * D_MODEL), 0.02),
        'w_in': nrm(ks[6], (DEPTH, D_MODEL, PROJ_WIDTH), D_MODEL ** -0.5),
        'conv_w': nrm(ks[7], (DEPTH, SSD_CONV, XBC_WIDTH), SSD_CONV ** -0.5),
        'conv_b': nrm(ks[8], (DEPTH, XBC_WIDTH), 0.02),
        'a_log': a_log,
        'dt_bias': dt_bias,
        'd_skip': 1.0 + nrm(ks[11], (DEPTH, 2, SSD_HEADS), 0.02),
        'rpb': nrm(ks[12], (DEPTH, ATT_HEADS, 2 * WIN_R - 1, 2 * WIN_C - 1), 0.02),
        'beta_attn': 1.0 + nrm(ks[13], (DEPTH, ATT_WIDTH), 0.02),
        'beta_ssm': 1.0 + nrm(ks[14], (DEPTH, SSD_WIDTH), 0.02),
        'w_out': nrm(ks[15], (DEPTH, D_MIX, D_MODEL), DEEPNORM_BETA * D_MIX ** -0.5),
        'ln1_g': 1.0 + nrm(ks[16], (DEPTH, D_MODEL), 0.02),
        'ln1_b': nrm(ks[17], (DEPTH, D_MODEL), 0.02),
        'peer_wq': nrm(ks[18], (DEPTH, D_MODEL, PEER_HEADS * PEER_QDIM), D_MODEL ** -0.5),
        'peer_keys': nrm(ks[19], (DEPTH, PEER_HEADS, 2, PEER_NKEYS, PEER_QDIM // 2), (PEER_QDIM // 2) ** -0.5),
        'peer_u': nrm(ks[20], (DEPTH, PEER_N, D_MODEL), D_MODEL ** -0.5),
        'peer_v': nrm(ks[21], (DEPTH, PEER_N, D_MODEL), DEEPNORM_BETA),
        'ln2_g': 1.0 + nrm(ks[22], (DEPTH, D_MODEL), 0.02),
        'ln2_b': nrm(ks[23], (DEPTH, D_MODEL), 0.02),
    }


def reference(x, c, ctx, c_ctx, ada_w, ada_b, w_in, conv_w, conv_b, a_log, dt_bias,
              d_skip, rpb, beta_attn, beta_ssm, w_out, ln1_g, ln1_b, peer_wq,
              peer_keys, peer_u, peer_v, ln2_g, ln2_b):
    xc = ctx
    silu_c = jax.nn.silu(c)
    silu_cc = jax.nn.silu(c_ctx)
    for layer in range(DEPTH):
        with_ctx = layer < DEPTH - 1
        sh1, sc1, g1, sh2, sc2, g2 = jnp.split(
            (silu_c @ ada_w[layer] + ada_b[layer])[:, None, :], N_MOD, axis=-1)
        csh1, csc1, cg1, csh2, csc2, cg2 = jnp.split(
            silu_cc @ ada_w[layer] + ada_b[layer], N_MOD, axis=-1)
        m_c, m_l = _mixer(xc * (1.0 + csc1) + csh1, x * (1.0 + sc1) + sh1,
                          w_in[layer], conv_w[layer], conv_b[layer], a_log[layer],
                          dt_bias[layer], d_skip[layer], rpb[layer], beta_attn[layer],
                          beta_ssm[layer], w_out[layer], with_ctx)
        x = _layer_norm(DEEPNORM_ALPHA * x + g1 * m_l, ln1_g[layer], ln1_b[layer])
        h_l = x * (1.0 + sc2) + sh2
        if with_ctx:
            xc = _layer_norm(DEEPNORM_ALPHA * xc + cg1 * m_c, ln1_g[layer], ln1_b[layer])
            h_c = xc * (1.0 + csc2) + csh2
            n_c = h_c.shape[0] * h_c.shape[1]
            f_all = _peer(jnp.concatenate([h_c.reshape(-1, D_MODEL), h_l.reshape(-1, D_MODEL)], axis=0),
                          peer_wq[layer], peer_keys[layer], peer_u[layer], peer_v[layer])
            f_c = f_all[:n_c].reshape(h_c.shape)
            f_l = f_all[n_c:].reshape(h_l.shape)
            xc = _layer_norm(DEEPNORM_ALPHA * xc + cg2 * f_c, ln2_g[layer], ln2_b[layer])
        else:
            f_l = _peer(h_l.reshape(-1, D_MODEL), peer_wq[layer], peer_keys[layer],
                        peer_u[layer], peer_v[layer]).reshape(h_l.shape)
        x = _layer_norm(DEEPNORM_ALPHA * x + g2 * f_l, ln2_g[layer], ln2_b[layer])
    return x
```

```python
import functools
import math

import numpy as np
import jax
import jax.numpy as jnp
from jax import lax
from jax.experimental import pallas as pl
from jax.experimental.pallas import tpu as pltpu

D_MODEL = 2048
DEPTH = 4
GRID_W = 64
ATT_HEADS = 8
ATT_HEAD_DIM = 128
ATT_WIDTH = ATT_HEADS * ATT_HEAD_DIM
WIN_R = 8
WIN_C = 16
ROPE_THETA = 10000.0
SSD_WIDTH = D_MODEL - ATT_WIDTH
SSD_HEAD_DIM = 64
SSD_HEADS = SSD_WIDTH // SSD_HEAD_DIM
SSD_GROUPS = 2
SSD_STATE = 128
SSD_CONV = 5
SSD_CHUNK = 128
XBC_WIDTH = SSD_WIDTH + 2 * SSD_GROUPS * SSD_STATE
PROJ_WIDTH = 3 * ATT_WIDTH + SSD_WIDTH + XBC_WIDTH + 2 * SSD_HEADS
PROJ_PAD = 5760
PEER_HEADS = 8
PEER_NKEYS = 128
PEER_QDIM = 256
PEER_TOPK = 16
PEER_SEL = PEER_HEADS * PEER_TOPK
DEEPNORM_ALPHA = (2 * DEPTH) ** 0.25
N_MOD = 6
EPS = 1e-6
NEG = -1e30

LANE = 128
F32 = jnp.float32
BF16 = jnp.bfloat16
HIGHEST = lax.Precision.HIGHEST

COL_Z = 3 * ATT_WIDTH
COL_XBC = COL_Z + SSD_WIDTH
COL_DT = COL_XBC + XBC_WIDTH


def _dot(a, b, precision=None):
    return jnp.dot(a, b, preferred_element_type=F32, precision=precision)


def _dot_nt(a, b):
    return lax.dot_general(a, b, (((1,), (1,)), ((), ())), preferred_element_type=F32)


def _silu(x):
    return x * (1.0 / (1.0 + jnp.exp(-x)))


def _layer_norm(x, g, b):
    mu = jnp.mean(x, axis=-1, keepdims=True)
    xc = x - mu
    var = jnp.mean(xc * xc, axis=-1, keepdims=True)
    return xc * lax.rsqrt(var + EPS) * g + b


def _rms_norm(x, g):
    return x * lax.rsqrt(jnp.mean(x * x, axis=-1, keepdims=True) + EPS) * g


def _ada_kernel(c_ref, w_ref, b_ref, o_ref):
    o_ref[0] = _dot(_silu(c_ref[...]), w_ref[0], HIGHEST) + b_ref[0]


def _ada(cond, ada_w, ada_b):
    depth, d, n = ada_w.shape
    r = cond.shape[0]
    tn = 1024
    return pl.pallas_call(
        _ada_kernel,
        grid=(depth, n // tn),
        in_specs=[
            pl.BlockSpec((r, d), lambda l, j: (0, 0)),
            pl.BlockSpec((1, d, tn), lambda l, j: (l, 0, j)),
            pl.BlockSpec((1, 1, tn), lambda l, j: (l, 0, j)),
        ],
        out_specs=pl.BlockSpec((1, r, tn), lambda l, j: (l, 0, j)),
        out_shape=jax.ShapeDtypeStruct((depth, r, n), F32),
        name="ada_mod",
    )(cond, ada_w, ada_b.reshape(depth, 1, n))


def _proj_kernel(x_ref, sh_ref, sc_ref, w_ref, o_ref):
    h = x_ref[0] * (1.0 + sc_ref[0]) + sh_ref[0]
    o_ref[0] = _dot(h.astype(BF16), w_ref[...])


def _mod_row(b, i, n_ctx_tiles, ctx_row):
    return jnp.where(i < n_ctx_tiles, ctx_row, b)


def _proj(x, mod, w, lc, ctx_row):
    bsz, s, d = x.shape
    n = w.shape[1]
    tm, tn = 256, 1152
    nct = lc // tm

    def mod_spec(k):
        return pl.BlockSpec((1, 1, d), lambda j, b, i: (_mod_row(b, i, nct, ctx_row), 0, k))

    return pl.pallas_call(
        _proj_kernel,
        grid=(n // tn, bsz, s // tm),
        in_specs=[
            pl.BlockSpec((1, tm, d), lambda j, b, i: (b, i, 0)),
            mod_spec(0), mod_spec(1),
            pl.BlockSpec((d, tn), lambda j, b, i: (0, j)),
        ],
        out_specs=pl.BlockSpec((1, tm, tn), lambda j, b, i: (b, i, j)),
        out_shape=jax.ShapeDtypeStruct((bsz, s, n), F32),
        name="in_proj",
    )(x, mod, mod, w)


def _softmax_parts(parts):
    m = parts[0].max(axis=-1, keepdims=True)
    for s in parts[1:]:
        m = jnp.maximum(m, s.max(axis=-1, keepdims=True))
    ps = [jnp.exp(s - m) for s in parts]
    l = ps[0].sum(axis=-1, keepdims=True)
    for p in ps[1:]:
        l = l + p.sum(axis=-1, keepdims=True)
    return ps, l


def _attn_kernel(q_ref, k_ref, v_ref, bias_ref, cos_ref, s1_ref, s2_ref, o_ref,
                 qs_ref, ks_ref, *, lc, rows, wr):
    scale = ATT_HEAD_DIM ** -0.5
    qc = (q_ref[0, :lc, :] * scale).astype(BF16)
    kc = k_ref[0, :lc, :].astype(BF16)
    vc = v_ref[0, :lc, :].astype(BF16)
    (p,), l = _softmax_parts([_dot_nt(qc, kc)])
    o_ref[0, :lc, :] = _dot(p.astype(BF16), vc) / l

    def rope(t):
        return (t * cos_ref[...] + pltpu.roll(t, 96, 1) * s1_ref[...]
                + pltpu.roll(t, 32, 1) * s2_ref[...])

    qs_ref[...] = (rope(q_ref[0, lc:, :]) * scale).astype(BF16)
    ks_ref[...] = rope(k_ref[0, lc:, :]).astype(BF16)

    def body(r, carry):
        rs = jnp.clip(r - WIN_R // 2, 0, rows - wr)
        q0 = pl.multiple_of(r * GRID_W, GRID_W)
        k0 = pl.multiple_of(rs * GRID_W, GRID_W)
        q_r = qs_ref[pl.ds(q0, GRID_W), :]
        k_nb = ks_ref[pl.ds(k0, wr * GRID_W), :]
        v_nb = v_ref[0, pl.ds(lc + k0, wr * GRID_W), :].astype(BF16)
        s_nb = _dot_nt(q_r, k_nb) + bias_ref[0, r]
        s_cx = _dot_nt(q_r, kc)
        (p_nb, p_cx), l = _softmax_parts([s_nb, s_cx])
        o = _dot(p_nb.astype(BF16), v_nb) + _dot(p_cx.astype(BF16), vc)
        o_ref[0, pl.ds(lc + q0, GRID_W), :] = o / l
        return carry

    lax.fori_loop(0, rows, body, 0)


def _attention(p, bias, cos, s1, s2, lc):
    bsz, s, _ = p.shape
    l = s - lc
    rows = l // GRID_W
    wr = min(WIN_R, rows)
    dh = ATT_HEAD_DIM

    def col(off):
        return pl.BlockSpec((1, s, dh), lambda b, h: (b, 0, off + h))

    tab = pl.BlockSpec((l, dh), lambda b, h: (0, 0))
    return pl.pallas_call(
        functools.partial(_attn_kernel, lc=lc, rows=rows, wr=wr),
        grid=(bsz, ATT_HEADS),
        in_specs=[
            col(0), col(ATT_HEADS), col(2 * ATT_HEADS),
            pl.BlockSpec((1, rows, GRID_W, wr * GRID_W), lambda b, h: (h, 0, 0, 0)),
            tab, tab, tab,
        ],
        out_specs=pl.BlockSpec((1, s, dh), lambda b, h: (b, 0, h)),
        out_shape=jax.ShapeDtypeStruct((bsz, s, ATT_WIDTH), F32),
        scratch_shapes=[pltpu.VMEM((l, dh), BF16), pltpu.VMEM((l, dh), BF16)],
        name="attention",
    )(p, p, p, bias, cos, s1, s2)


def _rope_tables(l):
    t = np.arange(l)
    pos = np.stack([t // GRID_W, t % GRID_W], axis=-1).astype(np.float32)
    axis_dim = ATT_HEAD_DIM // 2
    inv_freq = (ROPE_THETA ** (-np.arange(0, axis_dim, 2, dtype=np.float32) / axis_dim)).astype(np.float32)
    ang = pos[:, :, None] * inv_freq
    cos = np.cos(ang).astype(np.float32)
    sin = np.sin(ang).astype(np.float32)
    zero = np.zeros_like(sin)
    c = np.concatenate([cos, cos], axis=-1).reshape(l, ATT_HEAD_DIM)
    s1 = np.concatenate([-sin, zero], axis=-1).reshape(l, ATT_HEAD_DIM)
    s2 = np.concatenate([zero, sin], axis=-1).reshape(l, ATT_HEAD_DIM)
    return jnp.asarray(c), jnp.asarray(s1), jnp.asarray(s2)


def _bias_strips(rpb, rows):
    wr = min(WIN_R, rows)
    r_idx = np.arange(rows)
    c_idx = np.arange(GRID_W)
    row_start = np.clip(r_idx - WIN_R // 2, 0, rows - wr)
    key_rows = row_start[:, None] + np.arange(wr)[None, :]
    col_start = np.clip(c_idx - WIN_C // 2, 0, GRID_W - WIN_C)
    col_ok = (c_idx[None, :] >= col_start[:, None]) & (c_idx[None, :] < col_start[:, None] + WIN_C)
    dr = key_rows - r_idx[:, None] + (WIN_R - 1)
    dc = np.clip(c_idx[None, :] - c_idx[:, None] + (WIN_C - 1), 0, 2 * WIN_C - 2)
    bias = rpb[:, dr[:, None, :, None], dc[None, :, None, :]]
    bias = jnp.where(col_ok[None, None, :, None, :], bias, NEG)
    return bias.reshape(rpb.shape[0], rows, GRID_W, wr * GRID_W)


def _conv_kernel(x_ref, w_ref, b_ref, o_ref, *, lc):
    x = x_ref[0]
    s, tc = x.shape
    t = lax.broadcasted_iota(jnp.int32, (s, tc), 0)
    is_ctx = t < lc
    tin = jnp.where(is_ctx, t, t - lc)
    seg = jnp.where(is_ctx, lc, s - lc)
    acc = jnp.broadcast_to(b_ref[...], (s, tc))
    for k in range(SSD_CONV):
        delta = k - SSD_CONV // 2
        if delta == 0:
            acc = acc + x * w_ref[k:k + 1, :]
        else:
            shifted = pltpu.roll(x, (-delta) % s, 0)
            ok = (tin + delta >= 0) & (tin + delta < seg)
            acc = acc + jnp.where(ok, shifted, 0.0) * w_ref[k:k + 1, :]
    o_ref[0] = _silu(acc)


def _conv(p, conv_w, conv_b, lc):
    bsz, s, _ = p.shape
    tc = 256
    off = COL_XBC // tc
    return pl.pallas_call(
        functools.partial(_conv_kernel, lc=lc),
        grid=(bsz, XBC_WIDTH // tc),
        in_specs=[
            pl.BlockSpec((1, s, tc), lambda b, j: (b, 0, off + j)),
            pl.BlockSpec((SSD_CONV, tc), lambda b, j: (0, j)),
            pl.BlockSpec((1, tc), lambda b, j: (0, j)),
        ],
        out_specs=pl.BlockSpec((1, s, tc), lambda b, j: (b, 0, j)),
        out_shape=jax.ShapeDtypeStruct((bsz, s, XBC_WIDTH), F32),
        name="conv_silu",
    )(p, conv_w, conv_b.reshape(1, XBC_WIDTH))


def _ssd_kernel(x_ref, b_ref, c_ref, dt_ref, par_ref, o_ref, dt_s, st_s, *, lc):
    s = x_ref.shape[1]
    ck = SSD_CHUNK
    nc, ncc = s // ck, lc // ck
    hpg = SSD_HEADS // SSD_GROUPS
    g = pl.program_id(1)

    def norm(v):
        return jnp.where(g == 0, v, pltpu.roll(v, LANE - hpg, 1))

    par = norm(par_ref[...])
    dtb = par[0:1]
    a = -jnp.exp(par[1:2])
    dsk = par[2:3]
    z = norm(dt_ref[0]) + dtb
    dt_s[...] = jnp.maximum(z, 0.0) + jnp.log1p(jnp.exp(-jnp.abs(z)))

    ri = lax.broadcasted_iota(jnp.int32, (ck, ck), 0)
    ci = lax.broadcasted_iota(jnp.int32, (ck, ck), 1)

    for d in range(2):
        mask = (ci <= ri) if d == 0 else (ci >= ri)
        tri = mask.astype(F32)
        st_s[...] = jnp.zeros_like(st_s)

        def chunk_body(step, carry, d=d, mask=mask, tri=tri):
            if d == 0:
                c = step
            else:
                c = jnp.where(step < ncc, ncc - 1 - step, nc + ncc - 1 - step)
            r0 = pl.multiple_of(c * ck, ck)
            dtc = dt_s[pl.ds(r0, ck), :]
            cs = _dot(tri, dtc * a, HIGHEST)
            cs_t = cs.T
            bc = b_ref[0, pl.ds(r0, ck), :]
            cb = c_ref[0, pl.ds(r0, ck), :].astype(BF16)
            gmat = _dot_nt(cb, bc.astype(BF16))
            bt = bc.T.astype(BF16)
            for j in range(hpg):
                ln = d * SSD_HEADS + j
                col = cs[:, ln:ln + 1]
                row = cs_t[ln:ln + 1, :]
                lm = jnp.exp(jnp.where(mask, col - row, NEG))
                xh = x_ref[0, pl.ds(r0, ck), j * SSD_HEAD_DIM:(j + 1) * SSD_HEAD_DIM]
                xd = xh * dtc[:, ln:ln + 1]
                st = st_s[j]
                y = _dot((gmat * lm).astype(BF16), xd.astype(BF16))
                y = y + _dot(cb, st.astype(BF16)) * jnp.exp(col)
                tot = col[ck - 1:ck, :] if d == 0 else col[0:1, :]
                dec = jnp.exp(tot - col)
                st_s[j] = jnp.exp(tot) * st + _dot(bt, (xd * dec).astype(BF16))
                y = y + dsk[:, ln:ln + 1] * xh
                osl = (0, pl.ds(r0, ck), slice(j * SSD_HEAD_DIM, (j + 1) * SSD_HEAD_DIM))
                if d == 0:
                    o_ref[osl] = y
                else:
                    o_ref[osl] = o_ref[osl] + y
            return carry

        lax.fori_loop(0, nc, chunk_body, 0)


def _ssd(xbc, p, par, lc):
    bsz, s, _ = xbc.shape
    gw = SSD_WIDTH // SSD_GROUPS
    nb = SSD_WIDTH // SSD_STATE
    return pl.pallas_call(
        functools.partial(_ssd_kernel, lc=lc),
        grid=(bsz, SSD_GROUPS),
        in_specs=[
            pl.BlockSpec((1, s, gw), lambda b, g: (b, 0, g)),
            pl.BlockSpec((1, s, SSD_STATE), lambda b, g: (b, 0, nb + g)),
            pl.BlockSpec((1, s, SSD_STATE), lambda b, g: (b, 0, nb + SSD_GROUPS + g)),
            pl.BlockSpec((1, s, LANE), lambda b, g: (b, 0, COL_DT // LANE)),
            pl.BlockSpec((8, LANE), lambda b, g: (0, 0)),
        ],
        out_specs=pl.BlockSpec((1, s, gw), lambda b, g: (b, 0, g)),
        out_shape=jax.ShapeDtypeStruct((bsz, s, SSD_WIDTH), F32),
        scratch_shapes=[
            pltpu.VMEM((s, LANE), F32),
            pltpu.VMEM((SSD_HEADS // SSD_GROUPS, SSD_STATE, SSD_HEAD_DIM), F32),
        ],
        name="ssd_scan",
    )(xbc, xbc, xbc, p, par)


def _merge_kernel(att_ref, y_ref, z_ref, x_ref, g1_ref, sh2_ref, sc2_ref, ba_ref, bs_ref,
                  w_ref, lg_ref, lb_ref, x1_ref, h2_ref):
    att = _rms_norm(att_ref[0], ba_ref[...])
    ssm = _rms_norm(y_ref[0] * _silu(z_ref[0]), bs_ref[...])
    m = (_dot(att.astype(BF16), w_ref[:ATT_WIDTH, :])
         + _dot(ssm.astype(BF16), w_ref[ATT_WIDTH:, :]))
    xn = _layer_norm(DEEPNORM_ALPHA * x_ref[0] + g1_ref[0] * m, lg_ref[...], lb_ref[...])
    x1_ref[0] = xn
    h2_ref[0] = xn * (1.0 + sc2_ref[0]) + sh2_ref[0]


def _merge(att, y, p, x, mod, beta_a, beta_s, w_out, ln_g, ln_b, lc, ctx_row):
    bsz, s, d = x.shape
    tm = 256
    nct = lc // tm

    def mod_spec(k):
        return pl.BlockSpec((1, 1, d), lambda b, i: (_mod_row(b, i, nct, ctx_row), 0, k))

    def vec(n):
        return pl.BlockSpec((1, n), lambda b, i: (0, 0))

    half = pl.BlockSpec((1, tm, ATT_WIDTH), lambda b, i: (b, i, 0))
    full = pl.BlockSpec((1, tm, d), lambda b, i: (b, i, 0))
    return pl.pallas_call(
        _merge_kernel,
        grid=(bsz, s // tm),
        in_specs=[
            half, half,
            pl.BlockSpec((1, tm, SSD_WIDTH), lambda b, i: (b, i, COL_Z // SSD_WIDTH)),
            full, mod_spec(2), mod_spec(3), mod_spec(4),
            vec(ATT_WIDTH), vec(SSD_WIDTH),
            pl.BlockSpec((d, d), lambda b, i: (0, 0)),
            vec(d), vec(d),
        ],
        out_specs=[full, full],
        out_shape=[jax.ShapeDtypeStruct((bsz, s, d), F32)] * 2,
        name="merge_out_proj",
    )(att, y, p, x, mod, mod, mod, beta_a.reshape(1, -1), beta_s.reshape(1, -1), w_out,
      ln_g.reshape(1, d), ln_b.reshape(1, d))


def _topk_cols(s, k):
    n = s.shape[0]
    iota = lax.broadcasted_iota(jnp.int32, s.shape, 0)
    vals, idxs = [], []
    for _ in range(k):
        m = s.max(axis=0, keepdims=True)
        am = jnp.where(s == m, iota, n).min(axis=0, keepdims=True)
        vals.append(m)
        idxs.append(am)
        s = jnp.where(iota == am, -jnp.inf, s)
    return jnp.concatenate(vals, axis=0), jnp.concatenate(idxs, axis=0)


def _route_kernel(h_ref, wq_ref, keys_ref, idx_ref, gate_ref, q_s):
    hd = pl.program_id(1)
    half = PEER_QDIM // 2

    @pl.when(hd == 0)
    def _():
        q = _dot(h_ref[...].astype(BF16), wq_ref[...])
        for c in range(2 * PEER_HEADS):
            q_s[c] = q[:, c * half:(c + 1) * half].astype(BF16)

    tops, topi = [], []
    for a in range(2):
        kk = keys_ref[0, a].astype(BF16)
        v, i = _topk_cols(_dot_nt(kk, q_s[2 * hd + a]), PEER_TOPK)
        tops.append(v)
        topi.append(i)
    cand = jnp.concatenate([tops[0][i:i + 1] + tops[1] for i in range(PEER_TOPK)], axis=0)
    cidx = jnp.concatenate([topi[0][i:i + 1] * PEER_NKEYS + topi[1] for i in range(PEER_TOPK)], axis=0)
    iota = lax.broadcasted_iota(jnp.int32, cand.shape, 0)
    best, bidx = [], []
    for _ in range(PEER_TOPK):
        m = cand.max(axis=0, keepdims=True)
        pos = jnp.where(cand == m, iota, cand.shape[0]).min(axis=0, keepdims=True)
        sel = iota == pos
        best.append(m)
        bidx.append(jnp.where(sel, cidx, -1).max(axis=0, keepdims=True))
        cand = jnp.where(sel, -jnp.inf, cand)
    bs = jnp.concatenate(best, axis=0)
    e = jnp.exp(bs - bs[0:1])
    gate_ref[0] = e / e.sum(axis=0, keepdims=True)
    idx_ref[0] = jnp.concatenate(bidx, axis=0)


def _route(h2, wq, keys):
    t, d = h2.shape
    tm = 128
    half = PEER_QDIM // 2
    out = jax.ShapeDtypeStruct((PEER_HEADS, PEER_TOPK, t), jnp.int32)
    return pl.pallas_call(
        _route_kernel,
        grid=(t // tm, PEER_HEADS),
        in_specs=[
            pl.BlockSpec((tm, d), lambda i, h: (i, 0)),
            pl.BlockSpec((d, PEER_HEADS * PEER_QDIM), lambda i, h: (0, 0)),
            pl.BlockSpec((1, 2, PEER_NKEYS, half), lambda i, h: (h, 0, 0, 0)),
        ],
        out_specs=[pl.BlockSpec((1, PEER_TOPK, tm), lambda i, h: (h, 0, i))] * 2,
        out_shape=[out, jax.ShapeDtypeStruct(out.shape, F32)],
        scratch_shapes=[pltpu.VMEM((2 * PEER_HEADS, tm, half), BF16)],
        name="peer_route",
    )(h2, wq, keys)


PEER_TOKENS = 128
PEER_SLOTS = 4


def _gelu_tanh(x):
    return 0.5 * x * (1.0 + jnp.tanh(math.sqrt(2.0 / math.pi) * (x + 0.044715 * (x * x * x))))


def _peer_kernel(idx_hbm, gate_ref, h_ref, x1_ref, g2_ref, lg_ref, lb_ref, u_hbm, v_hbm, o_ref,
                 idx_s, ubuf, vbuf, f_s, isem, usem, vsem):
    i = pl.program_id(0)
    tt = PEER_TOKENS
    nsel = PEER_SEL
    icopy = pltpu.make_async_copy(idx_hbm.at[pl.ds(i * tt, tt), :], idx_s, isem)
    icopy.start()
    icopy.wait()

    def issue(t, slot):
        for e in range(nsel):
            row = idx_s[t, e]
            pltpu.make_async_copy(u_hbm.at[pl.ds(row, 1), :], ubuf.at[slot, pl.ds(e, 1), :],
                                  usem.at[slot]).start()
            pltpu.make_async_copy(v_hbm.at[pl.ds(row, 1), :], vbuf.at[slot, pl.ds(e, 1), :],
                                  vsem.at[slot]).start()

    def wait(slot):
        pltpu.make_async_copy(u_hbm.at[pl.ds(0, nsel), :], ubuf.at[slot], usem.at[slot]).wait()
        pltpu.make_async_copy(v_hbm.at[pl.ds(0, nsel), :], vbuf.at[slot], vsem.at[slot]).wait()

    for t in range(PEER_SLOTS - 1):
        issue(t, t)

    eye = (lax.broadcasted_iota(jnp.int32, (nsel, nsel), 0)
           == lax.broadcasted_iota(jnp.int32, (nsel, nsel), 1))

    def body(t, carry):
        slot = t % PEER_SLOTS
        nxt = t + PEER_SLOTS - 1

        @pl.when(nxt < tt)
        def _():
            issue(nxt, nxt % PEER_SLOTS)

        wait(slot)
        act = jnp.sum(ubuf[slot] * h_ref[pl.ds(t, 1), :], axis=1, keepdims=True)
        grow = jnp.broadcast_to(gate_ref[pl.ds(t, 1), :], (nsel, nsel))
        gcol = jnp.sum(jnp.where(eye, grow, 0.0), axis=1, keepdims=True)
        w = gcol * _gelu_tanh(act)
        f_s[pl.ds(t, 1), :] = jnp.sum(w * vbuf[slot], axis=0, keepdims=True)
        return carry

    lax.fori_loop(0, tt, body, 0)
    o_ref[...] = _layer_norm(DEEPNORM_ALPHA * x1_ref[...] + g2_ref[0] * f_s[...],
                             lg_ref[...], lb_ref[...])


def _peer(idx, gate, h2, x1, mod, ln_g, ln_b, u_tab, v_tab, s, lc, ctx_row):
    t, d = h2.shape
    tt = PEER_TOKENS
    tiles_per_b = s // tt
    nct = lc // tt

    def mod_row(i):
        return jnp.where(i % tiles_per_b < nct, ctx_row, i // tiles_per_b)

    tok = pl.BlockSpec((tt, d), lambda i: (i, 0))
    vec = pl.BlockSpec((1, d), lambda i: (0, 0))
    return pl.pallas_call(
        _peer_kernel,
        grid=(t // tt,),
        in_specs=[
            pl.BlockSpec(memory_space=pl.ANY),
            pl.BlockSpec((tt, PEER_SEL), lambda i: (i, 0)),
            tok, tok,
            pl.BlockSpec((1, 1, d), lambda i: (mod_row(i), 0, 5)),
            vec, vec,
            pl.BlockSpec(memory_space=pl.ANY),
            pl.BlockSpec(memory_space=pl.ANY),
        ],
        out_specs=tok,
        out_shape=jax.ShapeDtypeStruct((t, d), F32),
        scratch_shapes=[
            pltpu.SMEM((tt, PEER_SEL), jnp.int32),
            pltpu.VMEM((PEER_SLOTS, PEER_SEL, d), F32),
            pltpu.VMEM((PEER_SLOTS, PEER_SEL, d), F32),
            pltpu.VMEM((tt, d), F32),
            pltpu.SemaphoreType.DMA,
            pltpu.SemaphoreType.DMA((PEER_SLOTS,)),
            pltpu.SemaphoreType.DMA((PEER_SLOTS,)),
        ],
        name="peer_experts",
    )(idx, gate, h2, x1, mod, ln_g.reshape(1, d), ln_b.reshape(1, d), u_tab, v_tab)


def kernel(x, c, ctx, c_ctx, ada_w, ada_b, w_in, conv_w, conv_b, a_log, dt_bias, d_skip, rpb,
           beta_attn, beta_ssm, w_out, ln1_g, ln1_b, peer_wq, peer_keys, peer_u, peer_v,
           ln2_g, ln2_b):
    bsz, l, d = x.shape
    lc = ctx.shape[1]
    s = lc + l
    depth = w_in.shape[0]
    rows = l // GRID_W

    n_rows = -(-(bsz + 1) // 8) * 8
    cond = jnp.concatenate([c, c_ctx[None, :], jnp.zeros((n_rows - bsz - 1, d), F32)], axis=0)
    mods = _ada(cond, ada_w, ada_b).reshape(depth, n_rows, 1, N_MOD * d)
    cos, s1, s2 = _rope_tables(l)

    xa = jnp.concatenate([ctx, x], axis=1)
    for layer in range(depth):
        mod = mods[layer]
        w_in_p = jnp.pad(w_in[layer], ((0, 0), (0, PROJ_PAD - PROJ_WIDTH))).astype(BF16)
        p = _proj(xa, mod, w_in_p, lc, bsz)
        att = _attention(p, _bias_strips(rpb[layer], rows), cos, s1, s2, lc)
        xbc = _conv(p, conv_w[layer], conv_b[layer], lc)
        par = jnp.zeros((8, LANE), F32)
        par = par.at[0, :2 * SSD_HEADS].set(dt_bias[layer].reshape(-1))
        par = par.at[1, :2 * SSD_HEADS].set(a_log[layer].reshape(-1))
        par = par.at[2, :2 * SSD_HEADS].set(d_skip[layer].reshape(-1))
        y = _ssd(xbc, p, par, lc)
        x1, h2 = _merge(att, y, p, xa, mod, beta_attn[layer], beta_ssm[layer],
                        w_out[layer].astype(BF16), ln1_g[layer], ln1_b[layer], lc, bsz)
        h2f = h2.reshape(bsz * s, d)
        idx, gate = _route(h2f, peer_wq[layer].astype(BF16), peer_keys[layer])
        idx = idx.transpose(2, 0, 1).reshape(bsz * s, PEER_SEL)
        gate = gate.transpose(2, 0, 1).reshape(bsz * s, PEER_SEL)
        xa = _peer(idx, gate, h2f, x1.reshape(bsz * s, d), mod, ln2_g[layer], ln2_b[layer],
                   peer_u[layer], peer_v[layer], s, lc, bsz).reshape(bsz, s, d)
    return xa[:, lc:, :]
```

```python
import functools
import math

import numpy as np
import jax
import jax.numpy as jnp
from jax import lax
from jax.experimental import pallas as pl
from jax.experimental.pallas import tpu as pltpu

D_MODEL = 2048
DEPTH = 4
GRID_W = 64
ATT_HEADS = 8
ATT_HEAD_DIM = 128
ATT_WIDTH = ATT_HEADS * ATT_HEAD_DIM
WIN_R = 8
WIN_C = 16
ROPE_THETA = 10000.0
SSD_WIDTH = D_MODEL - ATT_WIDTH
SSD_HEAD_DIM = 64
SSD_HEADS = SSD_WIDTH // SSD_HEAD_DIM
SSD_GROUPS = 2
SSD_STATE = 128
SSD_CONV = 5
SSD_CHUNK = 128
XBC_WIDTH = SSD_WIDTH + 2 * SSD_GROUPS * SSD_STATE
PROJ_WIDTH = 3 * ATT_WIDTH + SSD_WIDTH + XBC_WIDTH + 2 * SSD_HEADS
PROJ_PAD = 5760
PEER_HEADS = 8
PEER_NKEYS = 128
PEER_QDIM = 256
PEER_TOPK = 16
PEER_SEL = PEER_HEADS * PEER_TOPK
DEEPNORM_ALPHA = (2 * DEPTH) ** 0.25
N_MOD = 6
EPS = 1e-6
NEG = -1e30

LANE = 128
F32 = jnp.float32
BF16 = jnp.bfloat16
HIGHEST = lax.Precision.HIGHEST

COL_Z = 3 * ATT_WIDTH
COL_XBC = COL_Z + SSD_WIDTH
COL_DT = COL_XBC + XBC_WIDTH


def _dot(a, b, precision=None):
    return jnp.dot(a, b, preferred_element_type=F32, precision=precision)


def _dot_nt(a, b):
    return lax.dot_general(a, b, (((1,), (1,)), ((), ())), preferred_element_type=F32)


def _silu(x):
    return x * (1.0 / (1.0 + jnp.exp(-x)))


def _layer_norm(x, g, b):
    mu = jnp.mean(x, axis=-1, keepdims=True)
    xc = x - mu
    var = jnp.mean(xc * xc, axis=-1, keepdims=True)
    return xc * lax.rsqrt(var + EPS) * g + b


def _rms_norm(x, g):
    return x * lax.rsqrt(jnp.mean(x * x, axis=-1, keepdims=True) + EPS) * g


def _ada_kernel(c_ref, w_ref, b_ref, o_ref):
    o_ref[0] = _dot(_silu(c_ref[...]), w_ref[0], HIGHEST) + b_ref[0]


def _ada(cond, ada_w, ada_b):
    depth, d, n = ada_w.shape
    r = cond.shape[0]
    tn = 1024
    return pl.pallas_call(
        _ada_kernel,
        grid=(depth, n // tn),
        in_specs=[
            pl.BlockSpec((r, d), lambda l, j: (0, 0)),
            pl.BlockSpec((1, d, tn), lambda l, j: (l, 0, j)),
            pl.BlockSpec((1, 1, tn), lambda l, j: (l, 0, j)),
        ],
        out_specs=pl.BlockSpec((1, r, tn), lambda l, j: (l, 0, j)),
        out_shape=jax.ShapeDtypeStruct((depth, r, n), F32),
        name="ada_mod",
    )(cond, ada_w, ada_b.reshape(depth, 1, n))


def _proj_kernel(x_ref, sh_ref, sc_ref, w_ref, o_ref):
    h = x_ref[0] * (1.0 + sc_ref[0]) + sh_ref[0]
    o_ref[0] = _dot(h.astype(BF16), w_ref[...])


def _mod_row(b, i, n_ctx_tiles, ctx_row):
    return jnp.where(i < n_ctx_tiles, ctx_row, b)


def _proj(x, mod, w, lc, ctx_row):
    bsz, s, d = x.shape
    n = w.shape[1]
    tm, tn = 256, 1152
    nct = lc // tm

    def mod_spec(k):
        return pl.BlockSpec((1, 1, d), lambda j, b, i: (_mod_row(b, i, nct, ctx_row), 0, k))

    return pl.pallas_call(
        _proj_kernel,
        grid=(n // tn, bsz, s // tm),
        in_specs=[
            pl.BlockSpec((1, tm, d), lambda j, b, i: (b, i, 0)),
            mod_spec(0), mod_spec(1),
            pl.BlockSpec((d, tn), lambda j, b, i: (0, j)),
        ],
        out_specs=pl.BlockSpec((1, tm, tn), lambda j, b, i: (b, i, j)),
        out_shape=jax.ShapeDtypeStruct((bsz, s, n), F32),
        name="in_proj",
    )(x, mod, mod, w)


def _softmax_parts(parts):
    m = parts[0].max(axis=-1, keepdims=True)
    for s in parts[1:]:
        m = jnp.maximum(m, s.max(axis=-1, keepdims=True))
    ps = [jnp.exp(s - m) for s in parts]
    l = ps[0].sum(axis=-1, keepdims=True)
    for p in ps[1:]:
        l = l + p.sum(axis=-1, keepdims=True)
    return ps, l


def _attn_kernel(q_ref, k_ref, v_ref, bias_ref, cos_ref, s1_ref, s2_ref, o_ref,
                 qs_ref, ks_ref, *, lc, rows, wr):
    scale = ATT_HEAD_DIM ** -0.5
    qc = (q_ref[0, :lc, :] * scale).astype(BF16)
    kc = k_ref[0, :lc, :].astype(BF16)
    vc = v_ref[0, :lc, :].astype(BF16)
    (p,), l = _softmax_parts([_dot_nt(qc, kc)])
    o_ref[0, :lc, :] = _dot(p.astype(BF16), vc) / l

    def rope(t):
        return (t * cos_ref[...] + pltpu.roll(t, 96, 1) * s1_ref[...]
                + pltpu.roll(t, 32, 1) * s2_ref[...])

    qs_ref[...] = (rope(q_ref[0, lc:, :]) * scale).astype(BF16)
    ks_ref[...] = rope(k_ref[0, lc:, :]).astype(BF16)

    def body(r, carry):
        rs = jnp.clip(r - WIN_R // 2, 0, rows - wr)
        q0 = pl.multiple_of(r * GRID_W, GRID_W)
        k0 = pl.multiple_of(rs * GRID_W, GRID_W)
        q_r = qs_ref[pl.ds(q0, GRID_W), :]
        k_nb = ks_ref[pl.ds(k0, wr * GRID_W), :]
        v_nb = v_ref[0, pl.ds(lc + k0, wr * GRID_W), :].astype(BF16)
        s_nb = _dot_nt(q_r, k_nb) + bias_ref[0, rs - r + (WIN_R - 1)]
        s_cx = _dot_nt(q_r, kc)
        (p_nb, p_cx), l = _softmax_parts([s_nb, s_cx])
        o = _dot(p_nb.astype(BF16), v_nb) + _dot(p_cx.astype(BF16), vc)
        o_ref[0, pl.ds(lc + q0, GRID_W), :] = o / l
        return carry

    lax.fori_loop(0, rows, body, 0)


def _attention(p, bias, cos, s1, s2, lc):
    bsz, s, _ = p.shape
    l = s - lc
    rows = l // GRID_W
    assert rows >= WIN_R and l % GRID_W == 0
    wr = WIN_R
    dh = ATT_HEAD_DIM

    def col(off):
        return pl.BlockSpec((1, s, dh), lambda b, h: (b, 0, off + h))

    tab = pl.BlockSpec((l, dh), lambda b, h: (0, 0))
    return pl.pallas_call(
        functools.partial(_attn_kernel, lc=lc, rows=rows, wr=wr),
        grid=(bsz, ATT_HEADS),
        in_specs=[
            col(0), col(ATT_HEADS), col(2 * ATT_HEADS),
            pl.BlockSpec((1, WIN_R, GRID_W, wr * GRID_W), lambda b, h: (h, 0, 0, 0)),
            tab, tab, tab,
        ],
        out_specs=pl.BlockSpec((1, s, dh), lambda b, h: (b, 0, h)),
        out_shape=jax.ShapeDtypeStruct((bsz, s, ATT_WIDTH), F32),
        scratch_shapes=[pltpu.VMEM((l, dh), BF16), pltpu.VMEM((l, dh), BF16)],
        name="attention",
    )(p, p, p, bias, cos, s1, s2)


def _rope_tables(l):
    t = np.arange(l)
    pos = np.stack([t // GRID_W, t % GRID_W], axis=-1).astype(np.float32)
    axis_dim = ATT_HEAD_DIM // 2
    inv_freq = (ROPE_THETA ** (-np.arange(0, axis_dim, 2, dtype=np.float32) / axis_dim)).astype(np.float32)
    ang = pos[:, :, None] * inv_freq
    cos = np.cos(ang).astype(np.float32)
    sin = np.sin(ang).astype(np.float32)
    zero = np.zeros_like(sin)
    c = np.concatenate([cos, cos], axis=-1).reshape(l, ATT_HEAD_DIM)
    s1 = np.concatenate([-sin, zero], axis=-1).reshape(l, ATT_HEAD_DIM)
    s2 = np.concatenate([zero, sin], axis=-1).reshape(l, ATT_HEAD_DIM)
    return jnp.asarray(c), jnp.asarray(s1), jnp.asarray(s2)


def _bias_strips(rpb):
    c_idx = np.arange(GRID_W)
    col_start = np.clip(c_idx - WIN_C // 2, 0, GRID_W - WIN_C)
    col_ok = (c_idx[None, :] >= col_start[:, None]) & (c_idx[None, :] < col_start[:, None] + WIN_C)
    dc = np.clip(c_idx[None, :] - c_idx[:, None] + (WIN_C - 1), 0, 2 * WIN_C - 2)
    h = rpb.shape[0]
    tab = jnp.take(rpb, jnp.asarray(dc.reshape(-1)), axis=2).reshape(h, 2 * WIN_R - 1, GRID_W, GRID_W)
    tab = jnp.where(col_ok[None, None], tab, NEG)
    strips = [tab[:, d0:d0 + WIN_R].transpose(0, 2, 1, 3).reshape(h, GRID_W, WIN_R * GRID_W)
              for d0 in range(WIN_R)]
    return jnp.stack(strips, axis=1)


def _conv_kernel(x_ref, w_ref, b_ref, o_ref, *, lc):
    x = x_ref[0]
    s, tc = x.shape
    t = lax.broadcasted_iota(jnp.int32, (s, tc), 0)
    is_ctx = t < lc
    tin = jnp.where(is_ctx, t, t - lc)
    seg = jnp.where(is_ctx, lc, s - lc)
    acc = jnp.broadcast_to(b_ref[...], (s, tc))
    for k in range(SSD_CONV):
        delta = k - SSD_CONV // 2
        if delta == 0:
            acc = acc + x * w_ref[k:k + 1, :]
        else:
            shifted = pltpu.roll(x, (-delta) % s, 0)
            ok = (tin + delta >= 0) & (tin + delta < seg)
            acc = acc + jnp.where(ok, shifted, 0.0) * w_ref[k:k + 1, :]
    o_ref[0] = _silu(acc)


def _conv(p, conv_w, conv_b, lc):
    bsz, s, _ = p.shape
    tc = 256
    off = COL_XBC // tc
    return pl.pallas_call(
        functools.partial(_conv_kernel, lc=lc),
        grid=(bsz, XBC_WIDTH // tc),
        in_specs=[
            pl.BlockSpec((1, s, tc), lambda b, j: (b, 0, off + j)),
            pl.BlockSpec((SSD_CONV, tc), lambda b, j: (0, j)),
            pl.BlockSpec((1, tc), lambda b, j: (0, j)),
        ],
        out_specs=pl.BlockSpec((1, s, tc), lambda b, j: (b, 0, j)),
        out_shape=jax.ShapeDtypeStruct((bsz, s, XBC_WIDTH), F32),
        name="conv_silu",
    )(p, conv_w, conv_b.reshape(1, XBC_WIDTH))


def _ssd_kernel(x_ref, b_ref, c_ref, dt_ref, par_ref, o_ref, dt_s, st_s, *, lc):
    s = x_ref.shape[1]
    ck = SSD_CHUNK
    nc, ncc = s // ck, lc // ck
    hpg = SSD_HEADS // SSD_GROUPS
    g = pl.program_id(1)

    def norm(v):
        return jnp.where(g == 0, v, pltpu.roll(v, LANE - hpg, 1))

    par = norm(par_ref[...])
    dtb = par[0:1]
    a = -jnp.exp(par[1:2])
    dsk = par[2:3]
    z = norm(dt_ref[0]) + dtb
    dt_s[...] = jnp.maximum(z, 0.0) + jnp.log1p(jnp.exp(-jnp.abs(z)))

    ri = lax.broadcasted_iota(jnp.int32, (ck, ck), 0)
    ci = lax.broadcasted_iota(jnp.int32, (ck, ck), 1)

    for d in range(2):
        mask = (ci <= ri) if d == 0 else (ci >= ri)
        tri = mask.astype(F32)
        st_s[...] = jnp.zeros_like(st_s)

        def chunk_body(step, carry, d=d, mask=mask, tri=tri):
            if d == 0:
                c = step
            else:
                c = jnp.where(step < ncc, ncc - 1 - step, nc + ncc - 1 - step)
            r0 = pl.multiple_of(c * ck, ck)
            dtc = dt_s[pl.ds(r0, ck), :]
            cs = _dot(tri, dtc * a, HIGHEST)
            cs_t = cs.T
            bc = b_ref[0, pl.ds(r0, ck), :]
            cb = c_ref[0, pl.ds(r0, ck), :].astype(BF16)
            gmat = _dot_nt(cb, bc.astype(BF16))
            bt = bc.T.astype(BF16)
            for j in range(hpg):
                ln = d * SSD_HEADS + j
                col = cs[:, ln:ln + 1]
                row = cs_t[ln:ln + 1, :]
                lm = jnp.exp(jnp.where(mask, col - row, NEG))
                xh = x_ref[0, pl.ds(r0, ck), j * SSD_HEAD_DIM:(j + 1) * SSD_HEAD_DIM]
                xd = xh * dtc[:, ln:ln + 1]
                st = st_s[j]
                y = _dot((gmat * lm).astype(BF16), xd.astype(BF16))
                y = y + _dot(cb, st.astype(BF16)) * jnp.exp(col)
                tot = col[ck - 1:ck, :] if d == 0 else col[0:1, :]
                dec = jnp.exp(tot - col)
                st_s[j] = jnp.exp(tot) * st + _dot(bt, (xd * dec).astype(BF16))
                y = y + dsk[:, ln:ln + 1] * xh
                osl = (0, pl.ds(r0, ck), slice(j * SSD_HEAD_DIM, (j + 1) * SSD_HEAD_DIM))
                if d == 0:
                    o_ref[osl] = y
                else:
                    o_ref[osl] = o_ref[osl] + y
            return carry

        lax.fori_loop(0, nc, chunk_body, 0)


def _ssd(xbc, p, par, lc):
    bsz, s, _ = xbc.shape
    gw = SSD_WIDTH // SSD_GROUPS
    nb = SSD_WIDTH // SSD_STATE
    return pl.pallas_call(
        functools.partial(_ssd_kernel, lc=lc),
        grid=(bsz, SSD_GROUPS),
        in_specs=[
            pl.BlockSpec((1, s, gw), lambda b, g: (b, 0, g)),
            pl.BlockSpec((1, s, SSD_STATE), lambda b, g: (b, 0, nb + g)),
            pl.BlockSpec((1, s, SSD_STATE), lambda b, g: (b, 0, nb + SSD_GROUPS + g)),
            pl.BlockSpec((1, s, LANE), lambda b, g: (b, 0, COL_DT // LANE)),
            pl.BlockSpec((8, LANE), lambda b, g: (0, 0)),
        ],
        out_specs=pl.BlockSpec((1, s, gw), lambda b, g: (b, 0, g)),
        out_shape=jax.ShapeDtypeStruct((bsz, s, SSD_WIDTH), F32),
        scratch_shapes=[
            pltpu.VMEM((s, LANE), F32),
            pltpu.VMEM((SSD_HEADS // SSD_GROUPS, SSD_STATE, SSD_HEAD_DIM), F32),
        ],
        name="ssd_scan",
    )(xbc, xbc, xbc, p, par)


def _merge_kernel(att_ref, y_ref, z_ref, x_ref, g1_ref, sh2_ref, sc2_ref, ba_ref, bs_ref,
                  w_ref, lg_ref, lb_ref, x1_ref, h2_ref):
    att = _rms_norm(att_ref[0], ba_ref[...])
    ssm = _rms_norm(y_ref[0] * _silu(z_ref[0]), bs_ref[...])
    m = (_dot(att.astype(BF16), w_ref[:ATT_WIDTH, :])
         + _dot(ssm.astype(BF16), w_ref[ATT_WIDTH:, :]))
    xn = _layer_norm(DEEPNORM_ALPHA * x_ref[0] + g1_ref[0] * m, lg_ref[...], lb_ref[...])
    x1_ref[0] = xn
    h2_ref[0] = xn * (1.0 + sc2_ref[0]) + sh2_ref[0]


def _merge(att, y, p, x, mod, beta_a, beta_s, w_out, ln_g, ln_b, lc, ctx_row):
    bsz, s, d = x.shape
    tm = 256
    nct = lc // tm

    def mod_spec(k):
        return pl.BlockSpec((1, 1, d), lambda b, i: (_mod_row(b, i, nct, ctx_row), 0, k))

    def vec(n):
        return pl.BlockSpec((1, n), lambda b, i: (0, 0))

    half = pl.BlockSpec((1, tm, ATT_WIDTH), lambda b, i: (b, i, 0))
    full = pl.BlockSpec((1, tm, d), lambda b, i: (b, i, 0))
    return pl.pallas_call(
        _merge_kernel,
        grid=(bsz, s // tm),
        in_specs=[
            half, half,
            pl.BlockSpec((1, tm, SSD_WIDTH), lambda b, i: (b, i, COL_Z // SSD_WIDTH)),
            full, mod_spec(2), mod_spec(3), mod_spec(4),
            vec(ATT_WIDTH), vec(SSD_WIDTH),
            pl.BlockSpec((d, d), lambda b, i: (0, 0)),
            vec(d), vec(d),
        ],
        out_specs=[full, full],
        out_shape=[jax.ShapeDtypeStruct((bsz, s, d), F32)] * 2,
        name="merge_out_proj",
    )(att, y, p, x, mod, mod, mod, beta_a.reshape(1, -1), beta_s.reshape(1, -1), w_out,
      ln_g.reshape(1, d), ln_b.reshape(1, d))


def _topk_cols(s, k):
    n = s.shape[0]
    iota = lax.broadcasted_iota(jnp.int32, s.shape, 0)
    vals, idxs = [], []
    for _ in range(k):
        m = s.max(axis=0, keepdims=True)
        am = jnp.where(s == m, iota, n).min(axis=0, keepdims=True)
        vals.append(m)
        idxs.append(am)
        s = jnp.where(iota == am, -jnp.inf, s)
    return jnp.concatenate(vals, axis=0), jnp.concatenate(idxs, axis=0)


def _route_kernel(h_ref, wq_ref, keys_ref, idx_ref, gate_ref, q_s):
    hd = pl.program_id(1)
    half = PEER_QDIM // 2

    @pl.when(hd == 0)
    def _():
        q = _dot(h_ref[...].astype(BF16), wq_ref[...])
        for c in range(2 * PEER_HEADS):
            q_s[c] = q[:, c * half:(c + 1) * half].astype(BF16)

    tops, topi = [], []
    for a in range(2):
        kk = keys_ref[0, a].astype(BF16)
        v, i = _topk_cols(_dot_nt(kk, q_s[2 * hd + a]), PEER_TOPK)
        tops.append(v)
        topi.append(i)
    nj = [PEER_TOPK // (i + 1) for i in range(PEER_TOPK)]
    pad = -sum(nj) % 8
    tm = tops[0].shape[1]
    cand = jnp.concatenate([tops[0][i:i + 1] + tops[1][:nj[i]] for i in range(PEER_TOPK)]
                           + [jnp.full((pad, tm), -jnp.inf, F32)], axis=0)
    cidx = jnp.concatenate([topi[0][i:i + 1] * PEER_NKEYS + topi[1][:nj[i]] for i in range(PEER_TOPK)]
                           + [jnp.full((pad, tm), -1, jnp.int32)], axis=0)
    iota = lax.broadcasted_iota(jnp.int32, cand.shape, 0)
    best, bidx = [], []
    for _ in range(PEER_TOPK):
        m = cand.max(axis=0, keepdims=True)
        pos = jnp.where(cand == m, iota, cand.shape[0]).min(axis=0, keepdims=True)
        sel = iota == pos
        best.append(m)
        bidx.append(jnp.where(sel, cidx, -1).max(axis=0, keepdims=True))
        cand = jnp.where(sel, -jnp.inf, cand)
    bs = jnp.concatenate(best, axis=0)
    e = jnp.exp(bs - bs[0:1])
    gate_ref[0] = e / e.sum(axis=0, keepdims=True)
    idx_ref[0] = jnp.concatenate(bidx, axis=0)


def _route(h2, wq, keys):
    t, d = h2.shape
    tm = 128
    half = PEER_QDIM // 2
    out = jax.ShapeDtypeStruct((PEER_HEADS, PEER_TOPK, t), jnp.int32)
    return pl.pallas_call(
        _route_kernel,
        grid=(t // tm, PEER_HEADS),
        in_specs=[
            pl.BlockSpec((tm, d), lambda i, h: (i, 0)),
            pl.BlockSpec((d, PEER_HEADS * PEER_QDIM), lambda i, h: (0, 0)),
            pl.BlockSpec((1, 2, PEER_NKEYS, half), lambda i, h: (h, 0, 0, 0)),
        ],
        out_specs=[pl.BlockSpec((1, PEER_TOPK, tm), lambda i, h: (h, 0, i))] * 2,
        out_shape=[out, jax.ShapeDtypeStruct(out.shape, F32)],
        scratch_shapes=[pltpu.VMEM((2 * PEER_HEADS, tm, half), BF16)],
        name="peer_route",
    )(h2, wq, keys)


PEER_TOKENS = 128
PEER_SLOTS = 8
PEER_PRIORITIES = (0, 1)


def _gelu_tanh(x):
    return 0.5 * x * (1.0 + jnp.tanh(math.sqrt(2.0 / math.pi) * (x + 0.044715 * (x * x * x))))


def _pack_tables(u_tab, v_tab):
    ub = lax.bitcast_convert_type(u_tab.astype(BF16), jnp.uint16).astype(jnp.uint32)
    vb = lax.bitcast_convert_type(v_tab.astype(BF16), jnp.uint16).astype(jnp.uint32)
    return ub | (vb << 16)


def _peer_kernel(idx_hbm, gate_ref, h_ref, x1_ref, g2_ref, lg_ref, lb_ref, tab_hbm, o_ref,
                 idx_s, buf, f_s, isem, sem):
    i = pl.program_id(0)
    n = pl.num_programs(0)
    tt = PEER_TOKENS
    nsel = PEER_SEL
    ns = PEER_SLOTS
    ahead = ns - 1
    cur = i % 2

    def idx_copy(tile, half):
        return pltpu.make_async_copy(idx_hbm.at[pl.ds(tile * tt, tt), :], idx_s.at[half],
                                     isem.at[half])

    def issue(half, t, slot):
        for e in range(nsel):
            row = idx_s[half, t, e]
            pltpu.make_async_copy(tab_hbm.at[row], buf.at[slot, pl.ds(e, 1), :],
                                  sem.at[slot]).start(priority=PEER_PRIORITIES[e % len(PEER_PRIORITIES)])

    def wait(slot):
        pltpu.make_async_copy(buf.at[slot], buf.at[slot], sem.at[slot]).wait()

    sub = 8
    groups = nsel // sub
    chunks = D_MODEL // LANE
    lane = lax.broadcasted_iota(jnp.int32, (sub, nsel), 1)
    subl = lax.broadcasted_iota(jnp.int32, (sub, nsel), 0)

    def consume(t, slot):
        wait(slot)
        hrow = h_ref[pl.ds(t, 1), :]
        grow = jnp.broadcast_to(gate_ref[pl.ds(t, 1), :], (sub, nsel))
        ws = []
        for g in range(groups):
            words = buf[slot, g * sub:(g + 1) * sub, :]
            prod = lax.bitcast_convert_type(words << 16, F32) * hrow
            acc = prod[:, :LANE]
            for c in range(1, chunks):
                acc = acc + prod[:, c * LANE:(c + 1) * LANE]
            act = jnp.sum(acc, axis=1, keepdims=True)
            gcol = jnp.sum(jnp.where(lane == subl + g * sub, grow, 0.0), axis=1, keepdims=True)
            ws.append(gcol * _gelu_tanh(act))
        f = None
        for g in range(groups):
            words = buf[slot, g * sub:(g + 1) * sub, :]
            term = ws[g] * lax.bitcast_convert_type(words & jnp.uint32(0xFFFF0000), F32)
            f = term if f is None else f + term
        f_s[pl.ds(t, 1), :] = jnp.sum(f, axis=0, keepdims=True)

    @pl.when(i == 0)
    def _():
        first = idx_copy(0, 0)
        first.start()
        first.wait()
        for t in range(ahead):
            issue(0, t, t)

    @pl.when(i + 1 < n)
    def _():
        idx_copy(i + 1, 1 - cur).start()

    def block(b, carry):
        for k in range(ns):
            t = b * ns + k
            consume(t, k)
            issue(cur, t + ahead, (k + ahead) % ns)
        return carry

    lax.fori_loop(0, tt // ns - 1, block, 0)

    @pl.when(i + 1 < n)
    def _():
        idx_copy(i + 1, 1 - cur).wait()

    for k in range(ns):
        t = tt - ns + k
        consume(t, k)
        if k == 0:
            issue(cur, t + ahead, (k + ahead) % ns)
        else:
            @pl.when(i + 1 < n)
            def _(k=k):
                issue(1 - cur, k - 1, (k + ahead) % ns)

    o_ref[...] = _layer_norm(DEEPNORM_ALPHA * x1_ref[...] + g2_ref[0] * f_s[...],
                             lg_ref[...], lb_ref[...])


def _peer(idx, gate, h2, x1, mod, ln_g, ln_b, tab, s, lc, ctx_row):
    t, d = h2.shape
    tt = PEER_TOKENS
    tiles_per_b = s // tt
    nct = lc // tt

    def mod_row(i):
        return jnp.where(i % tiles_per_b < nct, ctx_row, i // tiles_per_b)

    tok = pl.BlockSpec((tt, d), lambda i: (i, 0))
    vec = pl.BlockSpec((1, d), lambda i: (0, 0))
    return pl.pallas_call(
        _peer_kernel,
        grid=(t // tt,),
        in_specs=[
            pl.BlockSpec(memory_space=pl.ANY),
            pl.BlockSpec((tt, PEER_SEL), lambda i: (i, 0)),
            tok, tok,
            pl.BlockSpec((1, 1, d), lambda i: (mod_row(i), 0, 5)),
            vec, vec,
            pl.BlockSpec(memory_space=pl.ANY),
        ],
        out_specs=tok,
        out_shape=jax.ShapeDtypeStruct((t, d), F32),
        scratch_shapes=[
            pltpu.SMEM((2, tt, PEER_SEL), jnp.int32),
            pltpu.VMEM((PEER_SLOTS, PEER_SEL, d), jnp.uint32),
            pltpu.VMEM((tt, d), F32),
            pltpu.SemaphoreType.DMA((2,)),
            pltpu.SemaphoreType.DMA((PEER_SLOTS,)),
        ],
        compiler_params=pltpu.CompilerParams(dimension_semantics=("arbitrary",)),
        name="peer_experts",
    )(idx, gate, h2, x1, mod, ln_g.reshape(1, d), ln_b.reshape(1, d),
      tab.reshape(tab.shape[0], 1, d))


def kernel(x, c, ctx, c_ctx, ada_w, ada_b, w_in, conv_w, conv_b, a_log, dt_bias, d_skip, rpb,
           beta_attn, beta_ssm, w_out, ln1_g, ln1_b, peer_wq, peer_keys, peer_u, peer_v,
           ln2_g, ln2_b):
    bsz, l, d = x.shape
    lc = ctx.shape[1]
    s = lc + l
    depth = w_in.shape[0]
    rows = l // GRID_W

    n_rows = -(-(bsz + 1) // 8) * 8
    cond = jnp.concatenate([c, c_ctx[None, :], jnp.zeros((n_rows - bsz - 1, d), F32)], axis=0)
    mods = _ada(cond, ada_w, ada_b).reshape(depth, n_rows, 1, N_MOD * d)
    cos, s1, s2 = _rope_tables(l)

    xa = jnp.concatenate([ctx, x], axis=1)
    for layer in range(depth):
        mod = mods[layer]
        w_in_p = jnp.pad(w_in[layer], ((0, 0), (0, PROJ_PAD - PROJ_WIDTH))).astype(BF16)
        p = _proj(xa, mod, w_in_p, lc, bsz)
        att = _attention(p, _bias_strips(rpb[layer]), cos, s1, s2, lc)
        xbc = _conv(p, conv_w[layer], conv_b[layer], lc)
        par = jnp.zeros((8, LANE), F32)
        par = par.at[0, :2 * SSD_HEADS].set(dt_bias[layer].reshape(-1))
        par = par.at[1, :2 * SSD_HEADS].set(a_log[layer].reshape(-1))
        par = par.at[2, :2 * SSD_HEADS].set(d_skip[layer].reshape(-1))
        y = _ssd(xbc, p, par, lc)
        x1, h2 = _merge(att, y, p, xa, mod, beta_attn[layer], beta_ssm[layer],
                        w_out[layer].astype(BF16), ln1_g[layer], ln1_b[layer], lc, bsz)
        h2f = h2.reshape(bsz * s, d)
        idx, gate = _route(h2f, peer_wq[layer].astype(BF16), peer_keys[layer])
        idx = idx.transpose(2, 0, 1).reshape(bsz * s, PEER_SEL)
        gate = gate.transpose(2, 0, 1).reshape(bsz * s, PEER_SEL)
        xa = _peer(idx, gate, h2f, x1.reshape(bsz * s, d), mod, ln2_g[layer], ln2_b[layer],
                   _pack_tables(peer_u[layer], peer_v[layer]), s, lc, bsz).reshape(bsz, s, d)
    return xa[:, lc:, :]
```

```python
import functools
import math

import numpy as np
import jax
import jax.numpy as jnp
from jax import lax
from jax.experimental import pallas as pl
from jax.experimental.pallas import tpu as pltpu

D_MODEL = 2048
DEPTH = 4
GRID_W = 64
ATT_HEADS = 8
ATT_HEAD_DIM = 128
ATT_WIDTH = ATT_HEADS * ATT_HEAD_DIM
WIN_R = 8
WIN_C = 16
ROPE_THETA = 10000.0
SSD_WIDTH = D_MODEL - ATT_WIDTH
SSD_HEAD_DIM = 64
SSD_HEADS = SSD_WIDTH // SSD_HEAD_DIM
SSD_GROUPS = 2
SSD_STATE = 128
SSD_CONV = 5
SSD_CHUNK = 128
XBC_WIDTH = SSD_WIDTH + 2 * SSD_GROUPS * SSD_STATE
PROJ_WIDTH = 3 * ATT_WIDTH + SSD_WIDTH + XBC_WIDTH + 2 * SSD_HEADS
PROJ_PAD = 5760
PEER_HEADS = 8
PEER_NKEYS = 128
PEER_QDIM = 256
PEER_TOPK = 16
PEER_SEL = PEER_HEADS * PEER_TOPK
DEEPNORM_ALPHA = (2 * DEPTH) ** 0.25
N_MOD = 6
EPS = 1e-6
NEG = -1e30

LANE = 128
F32 = jnp.float32
BF16 = jnp.bfloat16
HIGHEST = lax.Precision.HIGHEST

COL_Z = 3 * ATT_WIDTH
COL_XBC = COL_Z + SSD_WIDTH
COL_DT = COL_XBC + XBC_WIDTH


def _dot(a, b, precision=None):
    return jnp.dot(a, b, preferred_element_type=F32, precision=precision)


def _dot_nt(a, b):
    return lax.dot_general(a, b, (((1,), (1,)), ((), ())), preferred_element_type=F32)


def _silu(x):
    return x * (1.0 / (1.0 + jnp.exp(-x)))


def _layer_norm(x, g, b):
    mu = jnp.mean(x, axis=-1, keepdims=True)
    xc = x - mu
    var = jnp.mean(xc * xc, axis=-1, keepdims=True)
    return xc * lax.rsqrt(var + EPS) * g + b


def _rms_norm(x, g):
    return x * lax.rsqrt(jnp.mean(x * x, axis=-1, keepdims=True) + EPS) * g


def _ada_kernel(c_ref, w_ref, b_ref, o_ref):
    o_ref[0] = _dot(_silu(c_ref[...]), w_ref[0], HIGHEST) + b_ref[0]


def _ada(cond, ada_w, ada_b):
    depth, d, n = ada_w.shape
    r = cond.shape[0]
    tn = 1024
    return pl.pallas_call(
        _ada_kernel,
        grid=(depth, n // tn),
        in_specs=[
            pl.BlockSpec((r, d), lambda l, j: (0, 0)),
            pl.BlockSpec((1, d, tn), lambda l, j: (l, 0, j)),
            pl.BlockSpec((1, 1, tn), lambda l, j: (l, 0, j)),
        ],
        out_specs=pl.BlockSpec((1, r, tn), lambda l, j: (l, 0, j)),
        out_shape=jax.ShapeDtypeStruct((depth, r, n), F32),
        name="ada_mod",
    )(cond, ada_w, ada_b.reshape(depth, 1, n))


def _proj_kernel(x_ref, sh_ref, sc_ref, w_ref, o_ref):
    h = x_ref[0] * (1.0 + sc_ref[0]) + sh_ref[0]
    o_ref[0] = _dot(h.astype(BF16), w_ref[...])


def _mod_row(b, i, n_ctx_tiles, ctx_row):
    return jnp.where(i < n_ctx_tiles, ctx_row, b)


def _proj(x, mod, w, lc, ctx_row):
    bsz, s, d = x.shape
    n = w.shape[1]
    tm, tn = 256, 1152
    nct = lc // tm

    def mod_spec(k):
        return pl.BlockSpec((1, 1, d), lambda j, b, i: (_mod_row(b, i, nct, ctx_row), 0, k))

    return pl.pallas_call(
        _proj_kernel,
        grid=(n // tn, bsz, s // tm),
        in_specs=[
            pl.BlockSpec((1, tm, d), lambda j, b, i: (b, i, 0)),
            mod_spec(0), mod_spec(1),
            pl.BlockSpec((d, tn), lambda j, b, i: (0, j)),
        ],
        out_specs=pl.BlockSpec((1, tm, tn), lambda j, b, i: (b, i, j)),
        out_shape=jax.ShapeDtypeStruct((bsz, s, n), F32),
        name="in_proj",
    )(x, mod, mod, w)


def _softmax_parts(parts):
    m = parts[0].max(axis=-1, keepdims=True)
    for s in parts[1:]:
        m = jnp.maximum(m, s.max(axis=-1, keepdims=True))
    ps = [jnp.exp(s - m) for s in parts]
    l = ps[0].sum(axis=-1, keepdims=True)
    for p in ps[1:]:
        l = l + p.sum(axis=-1, keepdims=True)
    return ps, l


def _attn_kernel(q_ref, k_ref, v_ref, bias_ref, cos_ref, s1_ref, s2_ref, o_ref,
                 qs_ref, ks_ref, *, lc, rows, wr):
    scale = ATT_HEAD_DIM ** -0.5
    qc = (q_ref[0, :lc, :] * scale).astype(BF16)
    kc = k_ref[0, :lc, :].astype(BF16)
    vc = v_ref[0, :lc, :].astype(BF16)
    (p,), l = _softmax_parts([_dot_nt(qc, kc)])
    o_ref[0, :lc, :] = _dot(p.astype(BF16), vc) / l

    def rope(t):
        return (t * cos_ref[...] + pltpu.roll(t, 96, 1) * s1_ref[...]
                + pltpu.roll(t, 32, 1) * s2_ref[...])

    qs_ref[...] = (rope(q_ref[0, lc:, :]) * scale).astype(BF16)
    ks_ref[...] = rope(k_ref[0, lc:, :]).astype(BF16)

    def body(r, carry):
        rs = jnp.clip(r - WIN_R // 2, 0, rows - wr)
        q0 = pl.multiple_of(r * GRID_W, GRID_W)
        k0 = pl.multiple_of(rs * GRID_W, GRID_W)
        q_r = qs_ref[pl.ds(q0, GRID_W), :]
        k_nb = ks_ref[pl.ds(k0, wr * GRID_W), :]
        v_nb = v_ref[0, pl.ds(lc + k0, wr * GRID_W), :].astype(BF16)
        s_nb = _dot_nt(q_r, k_nb) + bias_ref[0, rs - r + (WIN_R - 1)]
        s_cx = _dot_nt(q_r, kc)
        (p_nb, p_cx), l = _softmax_parts([s_nb, s_cx])
        o = _dot(p_nb.astype(BF16), v_nb) + _dot(p_cx.astype(BF16), vc)
        o_ref[0, pl.ds(lc + q0, GRID_W), :] = o / l
        return carry

    lax.fori_loop(0, rows, body, 0, unroll=4)


def _attention(p, bias, cos, s1, s2, lc):
    bsz, s, _ = p.shape
    l = s - lc
    rows = l // GRID_W
    assert rows >= WIN_R and l % GRID_W == 0
    wr = WIN_R
    dh = ATT_HEAD_DIM

    def col(off):
        return pl.BlockSpec((1, s, dh), lambda b, h: (b, 0, off + h))

    tab = pl.BlockSpec((l, dh), lambda b, h: (0, 0))
    return pl.pallas_call(
        functools.partial(_attn_kernel, lc=lc, rows=rows, wr=wr),
        grid=(bsz, ATT_HEADS),
        in_specs=[
            col(0), col(ATT_HEADS), col(2 * ATT_HEADS),
            pl.BlockSpec((1, WIN_R, GRID_W, wr * GRID_W), lambda b, h: (h, 0, 0, 0)),
            tab, tab, tab,
        ],
        out_specs=pl.BlockSpec((1, s, dh), lambda b, h: (b, 0, h)),
        out_shape=jax.ShapeDtypeStruct((bsz, s, ATT_WIDTH), F32),
        scratch_shapes=[pltpu.VMEM((l, dh), BF16), pltpu.VMEM((l, dh), BF16)],
        name="attention",
    )(p, p, p, bias, cos, s1, s2)


def _rope_tables(l):
    t = np.arange(l)
    pos = np.stack([t // GRID_W, t % GRID_W], axis=-1).astype(np.float32)
    axis_dim = ATT_HEAD_DIM // 2
    inv_freq = (ROPE_THETA ** (-np.arange(0, axis_dim, 2, dtype=np.float32) / axis_dim)).astype(np.float32)
    ang = pos[:, :, None] * inv_freq
    cos = np.cos(ang).astype(np.float32)
    sin = np.sin(ang).astype(np.float32)
    zero = np.zeros_like(sin)
    c = np.concatenate([cos, cos], axis=-1).reshape(l, ATT_HEAD_DIM)
    s1 = np.concatenate([-sin, zero], axis=-1).reshape(l, ATT_HEAD_DIM)
    s2 = np.concatenate([zero, sin], axis=-1).reshape(l, ATT_HEAD_DIM)
    return jnp.asarray(c), jnp.asarray(s1), jnp.asarray(s2)


def _bias_strips(rpb):
    c_idx = np.arange(GRID_W)
    col_start = np.clip(c_idx - WIN_C // 2, 0, GRID_W - WIN_C)
    col_ok = (c_idx[None, :] >= col_start[:, None]) & (c_idx[None, :] < col_start[:, None] + WIN_C)
    dc = np.clip(c_idx[None, :] - c_idx[:, None] + (WIN_C - 1), 0, 2 * WIN_C - 2)
    h = rpb.shape[0]
    tab = jnp.take(rpb, jnp.asarray(dc.reshape(-1)), axis=2).reshape(h, 2 * WIN_R - 1, GRID_W, GRID_W)
    tab = jnp.where(col_ok[None, None], tab, NEG)
    strips = [tab[:, d0:d0 + WIN_R].transpose(0, 2, 1, 3).reshape(h, GRID_W, WIN_R * GRID_W)
              for d0 in range(WIN_R)]
    return jnp.stack(strips, axis=1)


def _conv_kernel(x_ref, w_ref, b_ref, o_ref, *, lc):
    x = x_ref[0]
    s, tc = x.shape
    t = lax.broadcasted_iota(jnp.int32, (s, tc), 0)
    is_ctx = t < lc
    tin = jnp.where(is_ctx, t, t - lc)
    seg = jnp.where(is_ctx, lc, s - lc)
    acc = jnp.broadcast_to(b_ref[...], (s, tc))
    for k in range(SSD_CONV):
        delta = k - SSD_CONV // 2
        if delta == 0:
            acc = acc + x * w_ref[k:k + 1, :]
        else:
            shifted = pltpu.roll(x, (-delta) % s, 0)
            ok = (tin + delta >= 0) & (tin + delta < seg)
            acc = acc + jnp.where(ok, shifted, 0.0) * w_ref[k:k + 1, :]
    o_ref[0] = _silu(acc)


def _conv(p, conv_w, conv_b, lc):
    bsz, s, _ = p.shape
    tc = 256
    off = COL_XBC // tc
    return pl.pallas_call(
        functools.partial(_conv_kernel, lc=lc),
        grid=(bsz, XBC_WIDTH // tc),
        in_specs=[
            pl.BlockSpec((1, s, tc), lambda b, j: (b, 0, off + j)),
            pl.BlockSpec((SSD_CONV, tc), lambda b, j: (0, j)),
            pl.BlockSpec((1, tc), lambda b, j: (0, j)),
        ],
        out_specs=pl.BlockSpec((1, s, tc), lambda b, j: (b, 0, j)),
        out_shape=jax.ShapeDtypeStruct((bsz, s, XBC_WIDTH), F32),
        name="conv_silu",
    )(p, conv_w, conv_b.reshape(1, XBC_WIDTH))


def _ssd_kernel(x_ref, b_ref, c_ref, dt_ref, par_ref, o_ref, dt_s, st_s, *, lc):
    s = x_ref.shape[1]
    ck = SSD_CHUNK
    nc, ncc = s // ck, lc // ck
    hpg = SSD_HEADS // SSD_GROUPS
    g = pl.program_id(1)

    def norm(v):
        return jnp.where(g == 0, v, pltpu.roll(v, LANE - hpg, 1))

    par = norm(par_ref[...])
    dtb = par[0:1]
    a = -jnp.exp(par[1:2])
    dsk = par[2:3]
    z = norm(dt_ref[0]) + dtb
    dt_s[...] = jnp.maximum(z, 0.0) + jnp.log1p(jnp.exp(-jnp.abs(z)))

    ri = lax.broadcasted_iota(jnp.int32, (ck, ck), 0)
    ci = lax.broadcasted_iota(jnp.int32, (ck, ck), 1)

    for d in range(2):
        mask = (ci <= ri) if d == 0 else (ci >= ri)
        tri = mask.astype(F32)
        st_s[...] = jnp.zeros_like(st_s)

        def chunk_body(step, carry, d=d, mask=mask, tri=tri):
            if d == 0:
                c = step
            else:
                c = jnp.where(step < ncc, ncc - 1 - step, nc + ncc - 1 - step)
            r0 = pl.multiple_of(c * ck, ck)
            dtc = dt_s[pl.ds(r0, ck), :]
            cs = _dot(tri, dtc * a, HIGHEST)
            cs_t = cs.T
            bc = b_ref[0, pl.ds(r0, ck), :]
            cb = c_ref[0, pl.ds(r0, ck), :].astype(BF16)
            gmat = _dot_nt(cb, bc.astype(BF16))
            bt = bc.T.astype(BF16)
            for j in range(hpg):
                ln = d * SSD_HEADS + j
                col = cs[:, ln:ln + 1]
                row = cs_t[ln:ln + 1, :]
                lm = jnp.exp(jnp.where(mask, col - row, NEG))
                xh = x_ref[0, pl.ds(r0, ck), j * SSD_HEAD_DIM:(j + 1) * SSD_HEAD_DIM]
                xd = xh * dtc[:, ln:ln + 1]
                st = st_s[j]
                y = _dot((gmat * lm).astype(BF16), xd.astype(BF16))
                y = y + _dot(cb, st.astype(BF16)) * jnp.exp(col)
                tot = col[ck - 1:ck, :] if d == 0 else col[0:1, :]
                dec = jnp.exp(tot - col)
                st_s[j] = jnp.exp(tot) * st + _dot(bt, (xd * dec).astype(BF16))
                y = y + dsk[:, ln:ln + 1] * xh
                osl = (0, pl.ds(r0, ck), slice(j * SSD_HEAD_DIM, (j + 1) * SSD_HEAD_DIM))
                if d == 0:
                    o_ref[osl] = y
                else:
                    o_ref[osl] = o_ref[osl] + y
            return carry

        lax.fori_loop(0, nc, chunk_body, 0)


def _ssd(xbc, p, par, lc):
    bsz, s, _ = xbc.shape
    gw = SSD_WIDTH // SSD_GROUPS
    nb = SSD_WIDTH // SSD_STATE
    return pl.pallas_call(
        functools.partial(_ssd_kernel, lc=lc),
        grid=(bsz, SSD_GROUPS),
        in_specs=[
            pl.BlockSpec((1, s, gw), lambda b, g: (b, 0, g)),
            pl.BlockSpec((1, s, SSD_STATE), lambda b, g: (b, 0, nb + g)),
            pl.BlockSpec((1, s, SSD_STATE), lambda b, g: (b, 0, nb + SSD_GROUPS + g)),
            pl.BlockSpec((1, s, LANE), lambda b, g: (b, 0, COL_DT // LANE)),
            pl.BlockSpec((8, LANE), lambda b, g: (0, 0)),
        ],
        out_specs=pl.BlockSpec((1, s, gw), lambda b, g: (b, 0, g)),
        out_shape=jax.ShapeDtypeStruct((bsz, s, SSD_WIDTH), F32),
        scratch_shapes=[
            pltpu.VMEM((s, LANE), F32),
            pltpu.VMEM((SSD_HEADS // SSD_GROUPS, SSD_STATE, SSD_HEAD_DIM), F32),
        ],
        name="ssd_scan",
    )(xbc, xbc, xbc, p, par)


def _merge_kernel(att_ref, y_ref, z_ref, x_ref, g1_ref, sh2_ref, sc2_ref, ba_ref, bs_ref,
                  w_ref, lg_ref, lb_ref, x1_ref, h2_ref):
    att = _rms_norm(att_ref[0], ba_ref[...])
    ssm = _rms_norm(y_ref[0] * _silu(z_ref[0]), bs_ref[...])
    m = (_dot(att.astype(BF16), w_ref[:ATT_WIDTH, :])
         + _dot(ssm.astype(BF16), w_ref[ATT_WIDTH:, :]))
    xn = _layer_norm(DEEPNORM_ALPHA * x_ref[0] + g1_ref[0] * m, lg_ref[...], lb_ref[...])
    x1_ref[0] = xn
    h2_ref[0] = xn * (1.0 + sc2_ref[0]) + sh2_ref[0]


def _merge(att, y, p, x, mod, beta_a, beta_s, w_out, ln_g, ln_b, lc, ctx_row):
    bsz, s, d = x.shape
    tm = 256
    nct = lc // tm

    def mod_spec(k):
        return pl.BlockSpec((1, 1, d), lambda b, i: (_mod_row(b, i, nct, ctx_row), 0, k))

    def vec(n):
        return pl.BlockSpec((1, n), lambda b, i: (0, 0))

    half = pl.BlockSpec((1, tm, ATT_WIDTH), lambda b, i: (b, i, 0))
    full = pl.BlockSpec((1, tm, d), lambda b, i: (b, i, 0))
    return pl.pallas_call(
        _merge_kernel,
        grid=(bsz, s // tm),
        in_specs=[
            half, half,
            pl.BlockSpec((1, tm, SSD_WIDTH), lambda b, i: (b, i, COL_Z // SSD_WIDTH)),
            full, mod_spec(2), mod_spec(3), mod_spec(4),
            vec(ATT_WIDTH), vec(SSD_WIDTH),
            pl.BlockSpec((d, d), lambda b, i: (0, 0)),
            vec(d), vec(d),
        ],
        out_specs=[full, full],
        out_shape=[jax.ShapeDtypeStruct((bsz, s, d), F32)] * 2,
        name="merge_out_proj",
    )(att, y, p, x, mod, mod, mod, beta_a.reshape(1, -1), beta_s.reshape(1, -1), w_out,
      ln_g.reshape(1, d), ln_b.reshape(1, d))


def _topk_cols(s, k):
    n = s.shape[0]
    iota = lax.broadcasted_iota(jnp.int32, s.shape, 0)
    vals, idxs = [], []
    for _ in range(k):
        m = s.max(axis=0, keepdims=True)
        am = jnp.where(s == m, iota, n).min(axis=0, keepdims=True)
        vals.append(m)
        idxs.append(am)
        s = jnp.where(iota == am, -jnp.inf, s)
    return jnp.concatenate(vals, axis=0), jnp.concatenate(idxs, axis=0)


ROUTE_HEADS = 4


def _route_kernel(h_ref, wq_ref, keys_ref, idx_ref, gate_ref, q_s):
    hg = pl.program_id(1)
    half = PEER_QDIM // 2

    @pl.when(hg == 0)
    def _():
        q = _dot(h_ref[...].astype(BF16), wq_ref[...])
        for c in range(2 * PEER_HEADS):
            q_s[c] = q[:, c * half:(c + 1) * half].astype(BF16)

    for hh in range(ROUTE_HEADS):
        _route_head(hg * ROUTE_HEADS + hh, hh, keys_ref, idx_ref, gate_ref, q_s)


def _route_head(hd, hh, keys_ref, idx_ref, gate_ref, q_s):
    tops, topi = [], []
    for a in range(2):
        kk = keys_ref[hh, a].astype(BF16)
        v, i = _topk_cols(_dot_nt(kk, q_s[2 * hd + a]), PEER_TOPK)
        tops.append(v)
        topi.append(i)
    nj = [PEER_TOPK // (i + 1) for i in range(PEER_TOPK)]
    pad = -sum(nj) % 8
    tm = tops[0].shape[1]
    cand = jnp.concatenate([tops[0][i:i + 1] + tops[1][:nj[i]] for i in range(PEER_TOPK)]
                           + [jnp.full((pad, tm), -jnp.inf, F32)], axis=0)
    cidx = jnp.concatenate([topi[0][i:i + 1] * PEER_NKEYS + topi[1][:nj[i]] for i in range(PEER_TOPK)]
                           + [jnp.full((pad, tm), -1, jnp.int32)], axis=0)
    iota = lax.broadcasted_iota(jnp.int32, cand.shape, 0)
    best, bidx = [], []
    for _ in range(PEER_TOPK):
        m = cand.max(axis=0, keepdims=True)
        pos = jnp.where(cand == m, iota, cand.shape[0]).min(axis=0, keepdims=True)
        sel = iota == pos
        best.append(m)
        bidx.append(jnp.where(sel, cidx, -1).max(axis=0, keepdims=True))
        cand = jnp.where(sel, -jnp.inf, cand)
    bs = jnp.concatenate(best, axis=0)
    e = jnp.exp(bs - bs[0:1])
    gate_ref[hh] = e / e.sum(axis=0, keepdims=True)
    idx_ref[hh] = jnp.concatenate(bidx, axis=0)


def _route(h2, wq, keys):
    t, d = h2.shape
    tm = 128
    half = PEER_QDIM // 2
    out = jax.ShapeDtypeStruct((PEER_HEADS, PEER_TOPK, t), jnp.int32)
    return pl.pallas_call(
        _route_kernel,
        grid=(t // tm, PEER_HEADS // ROUTE_HEADS),
        in_specs=[
            pl.BlockSpec((tm, d), lambda i, h: (i, 0)),
            pl.BlockSpec((d, PEER_HEADS * PEER_QDIM), lambda i, h: (0, 0)),
            pl.BlockSpec((ROUTE_HEADS, 2, PEER_NKEYS, half), lambda i, h: (h, 0, 0, 0)),
        ],
        out_specs=[pl.BlockSpec((ROUTE_HEADS, PEER_TOPK, tm), lambda i, h: (h, 0, i))] * 2,
        out_shape=[out, jax.ShapeDtypeStruct(out.shape, F32)],
        scratch_shapes=[pltpu.VMEM((2 * PEER_HEADS, tm, half), BF16)],
        name="peer_route",
    )(h2, wq, keys)


PEER_TOKENS = 128
PEER_SLOTS = 8
PEER_PRIORITIES = (0, 1)


def _gelu_tanh(x):
    return 0.5 * x * (1.0 + jnp.tanh(math.sqrt(2.0 / math.pi) * (x + 0.044715 * (x * x * x))))


def _pack_kernel(u_ref, v_ref, o_ref):
    ub = lax.bitcast_convert_type(u_ref[...].astype(BF16).astype(F32), jnp.uint32)
    vb = lax.bitcast_convert_type(v_ref[...].astype(BF16).astype(F32), jnp.uint32)
    o_ref[...] = (ub >> 16) | vb


def _pack_tables(u_tab, v_tab):
    r, d = u_tab.shape
    tr = 256
    blk = pl.BlockSpec((tr, d), lambda i: (i, 0))
    return pl.pallas_call(
        _pack_kernel,
        grid=(r // tr,),
        in_specs=[blk, blk],
        out_specs=blk,
        out_shape=jax.ShapeDtypeStruct((r, d), jnp.uint32),
        name="peer_pack",
    )(u_tab, v_tab).reshape(r, 1, d)


def _peer_kernel(idx_hbm, gate_ref, h_ref, x1_ref, g2_ref, lg_ref, lb_ref, tab_hbm, o_ref,
                 idx_s, buf, f_s, isem, sem):
    i = pl.program_id(0)
    n = pl.num_programs(0)
    tt = PEER_TOKENS
    nsel = PEER_SEL
    ns = PEER_SLOTS
    ahead = ns - 1
    cur = i % 2

    def idx_copy(tile, half):
        return pltpu.make_async_copy(idx_hbm.at[pl.ds(tile * tt, tt), :], idx_s.at[half],
                                     isem.at[half])

    def issue(half, t, slot):
        for e in range(nsel):
            row = idx_s[half, t, e]
            pltpu.make_async_copy(tab_hbm.at[row], buf.at[slot, pl.ds(e, 1), :],
                                  sem.at[slot]).start(priority=PEER_PRIORITIES[e % len(PEER_PRIORITIES)])

    def wait(slot):
        pltpu.make_async_copy(buf.at[slot], buf.at[slot], sem.at[slot]).wait()

    sub = 8
    groups = nsel // sub
    chunks = D_MODEL // LANE
    lane = lax.broadcasted_iota(jnp.int32, (sub, nsel), 1)
    subl = lax.broadcasted_iota(jnp.int32, (sub, nsel), 0)

    def consume(t, slot):
        wait(slot)
        hrow = h_ref[pl.ds(t, 1), :]
        grow = jnp.broadcast_to(gate_ref[pl.ds(t, 1), :], (sub, nsel))
        ws = []
        for g in range(groups):
            words = buf[slot, g * sub:(g + 1) * sub, :]
            prod = lax.bitcast_convert_type(words << 16, F32) * hrow
            acc = prod[:, :LANE]
            for c in range(1, chunks):
                acc = acc + prod[:, c * LANE:(c + 1) * LANE]
            act = jnp.sum(acc, axis=1, keepdims=True)
            gcol = jnp.sum(jnp.where(lane == subl + g * sub, grow, 0.0), axis=1, keepdims=True)
            ws.append(gcol * _gelu_tanh(act))
        f = None
        for g in range(groups):
            words = buf[slot, g * sub:(g + 1) * sub, :]
            term = ws[g] * lax.bitcast_convert_type(words & jnp.uint32(0xFFFF0000), F32)
            f = term if f is None else f + term
        f_s[pl.ds(t, 1), :] = jnp.sum(f, axis=0, keepdims=True)

    @pl.when(i == 0)
    def _():
        first = idx_copy(0, 0)
        first.start()
        first.wait()
        for t in range(ahead):
            issue(0, t, t)

    @pl.when(i + 1 < n)
    def _():
        idx_copy(i + 1, 1 - cur).start()

    def block(b, carry):
        for k in range(ns):
            t = b * ns + k
            consume(t, k)
            issue(cur, t + ahead, (k + ahead) % ns)
        return carry

    lax.fori_loop(0, tt // ns - 1, block, 0)

    @pl.when(i + 1 < n)
    def _():
        idx_copy(i + 1, 1 - cur).wait()

    for k in range(ns):
        t = tt - ns + k
        consume(t, k)
        if k == 0:
            issue(cur, t + ahead, (k + ahead) % ns)
        else:
            @pl.when(i + 1 < n)
            def _(k=k):
                issue(1 - cur, k - 1, (k + ahead) % ns)

    o_ref[...] = _layer_norm(DEEPNORM_ALPHA * x1_ref[...] + g2_ref[0] * f_s[...],
                             lg_ref[...], lb_ref[...])


def _peer(idx, gate, h2, x1, mod, ln_g, ln_b, tab, s, lc, ctx_row):
    t, d = h2.shape
    tt = PEER_TOKENS
    tiles_per_b = s // tt
    nct = lc // tt

    def mod_row(i):
        return jnp.where(i % tiles_per_b < nct, ctx_row, i // tiles_per_b)

    tok = pl.BlockSpec((tt, d), lambda i: (i, 0))
    vec = pl.BlockSpec((1, d), lambda i: (0, 0))
    return pl.pallas_call(
        _peer_kernel,
        grid=(t // tt,),
        in_specs=[
            pl.BlockSpec(memory_space=pl.ANY),
            pl.BlockSpec((tt, PEER_SEL), lambda i: (i, 0)),
            tok, tok,
            pl.BlockSpec((1, 1, d), lambda i: (mod_row(i), 0, 5)),
            vec, vec,
            pl.BlockSpec(memory_space=pl.ANY),
        ],
        out_specs=tok,
        out_shape=jax.ShapeDtypeStruct((t, d), F32),
        scratch_shapes=[
            pltpu.SMEM((2, tt, PEER_SEL), jnp.int32),
            pltpu.VMEM((PEER_SLOTS, PEER_SEL, d), jnp.uint32),
            pltpu.VMEM((tt, d), F32),
            pltpu.SemaphoreType.DMA((2,)),
            pltpu.SemaphoreType.DMA((PEER_SLOTS,)),
        ],
        compiler_params=pltpu.CompilerParams(dimension_semantics=("arbitrary",)),
        name="peer_experts",
    )(idx, gate, h2, x1, mod, ln_g.reshape(1, d), ln_b.reshape(1, d), tab)


def kernel(x, c, ctx, c_ctx, ada_w, ada_b, w_in, conv_w, conv_b, a_log, dt_bias, d_skip, rpb,
           beta_attn, beta_ssm, w_out, ln1_g, ln1_b, peer_wq, peer_keys, peer_u, peer_v,
           ln2_g, ln2_b):
    bsz, l, d = x.shape
    lc = ctx.shape[1]
    s = lc + l
    depth = w_in.shape[0]
    rows = l // GRID_W

    n_rows = -(-(bsz + 1) // 8) * 8
    cond = jnp.concatenate([c, c_ctx[None, :], jnp.zeros((n_rows - bsz - 1, d), F32)], axis=0)
    mods = _ada(cond, ada_w, ada_b).reshape(depth, n_rows, 1, N_MOD * d)
    cos, s1, s2 = _rope_tables(l)
    n_exp = peer_u.shape[1]
    tab = _pack_tables(peer_u.reshape(depth * n_exp, d), peer_v.reshape(depth * n_exp, d))

    xa = jnp.concatenate([ctx, x], axis=1)
    for layer in range(depth):
        mod = mods[layer]
        w_in_p = jnp.pad(w_in[layer], ((0, 0), (0, PROJ_PAD - PROJ_WIDTH))).astype(BF16)
        p = _proj(xa, mod, w_in_p, lc, bsz)
        att = _attention(p, _bias_strips(rpb[layer]), cos, s1, s2, lc)
        xbc = _conv(p, conv_w[layer], conv_b[layer], lc)
        par = jnp.zeros((8, LANE), F32)
        par = par.at[0, :2 * SSD_HEADS].set(dt_bias[layer].reshape(-1))
        par = par.at[1, :2 * SSD_HEADS].set(a_log[layer].reshape(-1))
        par = par.at[2, :2 * SSD_HEADS].set(d_skip[layer].reshape(-1))
        y = _ssd(xbc, p, par, lc)
        x1, h2 = _merge(att, y, p, xa, mod, beta_attn[layer], beta_ssm[layer],
                        w_out[layer].astype(BF16), ln1_g[layer], ln1_b[layer], lc, bsz)
        last = layer == depth - 1
        s_t, lc_t = (l, 0) if last else (s, lc)
        if last:
            h2, x1 = h2[:, lc:, :], x1[:, lc:, :]
        h2f = h2.reshape(bsz * s_t, d)
        idx, gate = _route(h2f, peer_wq[layer].astype(BF16), peer_keys[layer])
        idx = idx.transpose(2, 0, 1).reshape(bsz * s_t, PEER_SEL) + layer * n_exp
        gate = gate.transpose(2, 0, 1).reshape(bsz * s_t, PEER_SEL)
        xa = _peer(idx, gate, h2f, x1.reshape(bsz * s_t, d), mod, ln2_g[layer], ln2_b[layer],
                   tab, s_t, lc_t, bsz).reshape(bsz, s_t, d)
    return xa
```

```python
import functools
import math

import numpy as np
import jax
import jax.numpy as jnp
from jax import lax
from jax.experimental import pallas as pl
from jax.experimental.pallas import tpu as pltpu

D_MODEL = 2048
DEPTH = 4
GRID_W = 64
ATT_HEADS = 8
ATT_HEAD_DIM = 128
ATT_WIDTH = ATT_HEADS * ATT_HEAD_DIM
WIN_R = 8
WIN_C = 16
ROPE_THETA = 10000.0
SSD_WIDTH = D_MODEL - ATT_WIDTH
SSD_HEAD_DIM = 64
SSD_HEADS = SSD_WIDTH // SSD_HEAD_DIM
SSD_GROUPS = 2
SSD_STATE = 128
SSD_CONV = 5
SSD_CHUNK = 128
XBC_WIDTH = SSD_WIDTH + 2 * SSD_GROUPS * SSD_STATE
PROJ_WIDTH = 3 * ATT_WIDTH + SSD_WIDTH + XBC_WIDTH + 2 * SSD_HEADS
PROJ_PAD = 5760
PEER_HEADS = 8
PEER_NKEYS = 128
PEER_QDIM = 256
PEER_TOPK = 16
PEER_SEL = PEER_HEADS * PEER_TOPK
DEEPNORM_ALPHA = (2 * DEPTH) ** 0.25
N_MOD = 6
EPS = 1e-6
NEG = -1e30

LANE = 128
F32 = jnp.float32
BF16 = jnp.bfloat16
HIGHEST = lax.Precision.HIGHEST

COL_Z = 3 * ATT_WIDTH
COL_XBC = COL_Z + SSD_WIDTH
COL_DT = COL_XBC + XBC_WIDTH


def _dot(a, b, precision=None):
    return jnp.dot(a, b, preferred_element_type=F32, precision=precision)


def _dot_nt(a, b):
    return lax.dot_general(a, b, (((1,), (1,)), ((), ())), preferred_element_type=F32)


def _silu(x):
    return x * (1.0 / (1.0 + jnp.exp(-x)))


def _layer_norm(x, g, b):
    mu = jnp.mean(x, axis=-1, keepdims=True)
    xc = x - mu
    var = jnp.mean(xc * xc, axis=-1, keepdims=True)
    return xc * lax.rsqrt(var + EPS) * g + b


def _rms_norm(x, g):
    return x * lax.rsqrt(jnp.mean(x * x, axis=-1, keepdims=True) + EPS) * g


def _ada_kernel(c_ref, w_ref, b_ref, o_ref):
    o_ref[0] = _dot(_silu(c_ref[...]), w_ref[0], HIGHEST) + b_ref[0]


def _ada(cond, ada_w, ada_b):
    depth, d, n = ada_w.shape
    r = cond.shape[0]
    tn = 1024
    return pl.pallas_call(
        _ada_kernel,
        grid=(depth, n // tn),
        in_specs=[
            pl.BlockSpec((r, d), lambda l, j: (0, 0)),
            pl.BlockSpec((1, d, tn), lambda l, j: (l, 0, j)),
            pl.BlockSpec((1, 1, tn), lambda l, j: (l, 0, j)),
        ],
        out_specs=pl.BlockSpec((1, r, tn), lambda l, j: (l, 0, j)),
        out_shape=jax.ShapeDtypeStruct((depth, r, n), F32),
        name="ada_mod",
    )(cond, ada_w, ada_b.reshape(depth, 1, n))


def _proj_kernel(x_ref, sh_ref, sc_ref, w_ref, o_ref):
    h = x_ref[0] * (1.0 + sc_ref[0]) + sh_ref[0]
    o_ref[0] = _dot(h.astype(BF16), w_ref[...])


def _mod_row(b, i, n_ctx_tiles, ctx_row):
    return jnp.where(i < n_ctx_tiles, ctx_row, b)


def _proj(x, mod, w, lc, ctx_row):
    bsz, s, d = x.shape
    n = w.shape[1]
    tm, tn = 256, 1152
    nct = lc // tm

    def mod_spec(k):
        return pl.BlockSpec((1, 1, d), lambda j, b, i: (_mod_row(b, i, nct, ctx_row), 0, k))

    return pl.pallas_call(
        _proj_kernel,
        grid=(n // tn, bsz, s // tm),
        in_specs=[
            pl.BlockSpec((1, tm, d), lambda j, b, i: (b, i, 0)),
            mod_spec(0), mod_spec(1),
            pl.BlockSpec((d, tn), lambda j, b, i: (0, j)),
        ],
        out_specs=pl.BlockSpec((1, tm, tn), lambda j, b, i: (b, i, j)),
        out_shape=jax.ShapeDtypeStruct((bsz, s, n), F32),
        name="in_proj",
    )(x, mod, mod, w)


def _softmax_parts(parts):
    m = parts[0].max(axis=-1, keepdims=True)
    for s in parts[1:]:
        m = jnp.maximum(m, s.max(axis=-1, keepdims=True))
    ps = [jnp.exp(s - m) for s in parts]
    l = ps[0].sum(axis=-1, keepdims=True)
    for p in ps[1:]:
        l = l + p.sum(axis=-1, keepdims=True)
    return ps, l


def _attn_kernel(q_ref, k_ref, v_ref, bias_ref, cos_ref, s1_ref, s2_ref, o_ref,
                 qs_ref, ks_ref, *, lc, rows, wr):
    scale = ATT_HEAD_DIM ** -0.5
    qc = (q_ref[0, :lc, :] * scale).astype(BF16)
    kc = k_ref[0, :lc, :].astype(BF16)
    vc = v_ref[0, :lc, :].astype(BF16)
    (p,), l = _softmax_parts([_dot_nt(qc, kc)])
    o_ref[0, :lc, :] = _dot(p.astype(BF16), vc) / l

    def rope(t):
        return (t * cos_ref[...] + pltpu.roll(t, 96, 1) * s1_ref[...]
                + pltpu.roll(t, 32, 1) * s2_ref[...])

    qs_ref[...] = (rope(q_ref[0, lc:, :]) * scale).astype(BF16)
    ks_ref[...] = rope(k_ref[0, lc:, :]).astype(BF16)

    def body(r, carry):
        rs = jnp.clip(r - WIN_R // 2, 0, rows - wr)
        q0 = pl.multiple_of(r * GRID_W, GRID_W)
        k0 = pl.multiple_of(rs * GRID_W, GRID_W)
        q_r = qs_ref[pl.ds(q0, GRID_W), :]
        k_nb = ks_ref[pl.ds(k0, wr * GRID_W), :]
        v_nb = v_ref[0, pl.ds(lc + k0, wr * GRID_W), :].astype(BF16)
        s_nb = _dot_nt(q_r, k_nb) + bias_ref[0, rs - r + (WIN_R - 1)]
        s_cx = _dot_nt(q_r, kc)
        (p_nb, p_cx), l = _softmax_parts([s_nb, s_cx])
        o = _dot(p_nb.astype(BF16), v_nb) + _dot(p_cx.astype(BF16), vc)
        o_ref[0, pl.ds(lc + q0, GRID_W), :] = o / l
        return carry

    lax.fori_loop(0, rows, body, 0, unroll=4)


def _attention(p, bias, cos, s1, s2, lc):
    bsz, s, _ = p.shape
    l = s - lc
    rows = l // GRID_W
    assert rows >= WIN_R and l % GRID_W == 0
    wr = WIN_R
    dh = ATT_HEAD_DIM

    def col(off):
        return pl.BlockSpec((1, s, dh), lambda b, h: (b, 0, off + h))

    tab = pl.BlockSpec((l, dh), lambda b, h: (0, 0))
    return pl.pallas_call(
        functools.partial(_attn_kernel, lc=lc, rows=rows, wr=wr),
        grid=(bsz, ATT_HEADS),
        in_specs=[
            col(0), col(ATT_HEADS), col(2 * ATT_HEADS),
            pl.BlockSpec((1, WIN_R, GRID_W, wr * GRID_W), lambda b, h: (h, 0, 0, 0)),
            tab, tab, tab,
        ],
        out_specs=pl.BlockSpec((1, s, dh), lambda b, h: (b, 0, h)),
        out_shape=jax.ShapeDtypeStruct((bsz, s, ATT_WIDTH), F32),
        scratch_shapes=[pltpu.VMEM((l, dh), BF16), pltpu.VMEM((l, dh), BF16)],
        name="attention",
    )(p, p, p, bias, cos, s1, s2)


def _rope_tables(l):
    t = np.arange(l)
    pos = np.stack([t // GRID_W, t % GRID_W], axis=-1).astype(np.float32)
    axis_dim = ATT_HEAD_DIM // 2
    inv_freq = (ROPE_THETA ** (-np.arange(0, axis_dim, 2, dtype=np.float32) / axis_dim)).astype(np.float32)
    ang = pos[:, :, None] * inv_freq
    cos = np.cos(ang).astype(np.float32)
    sin = np.sin(ang).astype(np.float32)
    zero = np.zeros_like(sin)
    c = np.concatenate([cos, cos], axis=-1).reshape(l, ATT_HEAD_DIM)
    s1 = np.concatenate([-sin, zero], axis=-1).reshape(l, ATT_HEAD_DIM)
    s2 = np.concatenate([zero, sin], axis=-1).reshape(l, ATT_HEAD_DIM)
    return jnp.asarray(c), jnp.asarray(s1), jnp.asarray(s2)


def _bias_strips(rpb):
    c_idx = np.arange(GRID_W)
    col_start = np.clip(c_idx - WIN_C // 2, 0, GRID_W - WIN_C)
    col_ok = (c_idx[None, :] >= col_start[:, None]) & (c_idx[None, :] < col_start[:, None] + WIN_C)
    dc = np.clip(c_idx[None, :] - c_idx[:, None] + (WIN_C - 1), 0, 2 * WIN_C - 2)
    h = rpb.shape[0]
    tab = jnp.take(rpb, jnp.asarray(dc.reshape(-1)), axis=2).reshape(h, 2 * WIN_R - 1, GRID_W, GRID_W)
    tab = jnp.where(col_ok[None, None], tab, NEG)
    strips = [tab[:, d0:d0 + WIN_R].transpose(0, 2, 1, 3).reshape(h, GRID_W, WIN_R * GRID_W)
              for d0 in range(WIN_R)]
    return jnp.stack(strips, axis=1)


def _conv_kernel(x_ref, w_ref, b_ref, o_ref, *, lc):
    x = x_ref[0]
    s, tc = x.shape
    t = lax.broadcasted_iota(jnp.int32, (s, tc), 0)
    is_ctx = t < lc
    tin = jnp.where(is_ctx, t, t - lc)
    seg = jnp.where(is_ctx, lc, s - lc)
    acc = jnp.broadcast_to(b_ref[...], (s, tc))
    for k in range(SSD_CONV):
        delta = k - SSD_CONV // 2
        if delta == 0:
            acc = acc + x * w_ref[k:k + 1, :]
        else:
            shifted = pltpu.roll(x, (-delta) % s, 0)
            ok = (tin + delta >= 0) & (tin + delta < seg)
            acc = acc + jnp.where(ok, shifted, 0.0) * w_ref[k:k + 1, :]
    o_ref[0] = _silu(acc)


def _conv(p, conv_w, conv_b, lc):
    bsz, s, _ = p.shape
    tc = 256
    off = COL_XBC // tc
    return pl.pallas_call(
        functools.partial(_conv_kernel, lc=lc),
        grid=(bsz, XBC_WIDTH // tc),
        in_specs=[
            pl.BlockSpec((1, s, tc), lambda b, j: (b, 0, off + j)),
            pl.BlockSpec((SSD_CONV, tc), lambda b, j: (0, j)),
            pl.BlockSpec((1, tc), lambda b, j: (0, j)),
        ],
        out_specs=pl.BlockSpec((1, s, tc), lambda b, j: (b, 0, j)),
        out_shape=jax.ShapeDtypeStruct((bsz, s, XBC_WIDTH), F32),
        name="conv_silu",
    )(p, conv_w, conv_b.reshape(1, XBC_WIDTH))


def _ssd_kernel(x_ref, b_ref, c_ref, dt_ref, par_ref, o_ref, dt_s, st_s, *, lc):
    s = x_ref.shape[1]
    ck = SSD_CHUNK
    nc, ncc = s // ck, lc // ck
    hpg = SSD_HEADS // SSD_GROUPS
    g = pl.program_id(1)

    def norm(v):
        return jnp.where(g == 0, v, pltpu.roll(v, LANE - hpg, 1))

    par = norm(par_ref[...])
    dtb = par[0:1]
    a = -jnp.exp(par[1:2])
    dsk = par[2:3]
    z = norm(dt_ref[0]) + dtb
    dt_s[...] = jnp.maximum(z, 0.0) + jnp.log1p(jnp.exp(-jnp.abs(z)))

    ri = lax.broadcasted_iota(jnp.int32, (ck, ck), 0)
    ci = lax.broadcasted_iota(jnp.int32, (ck, ck), 1)

    for d in range(2):
        mask = (ci <= ri) if d == 0 else (ci >= ri)
        tri = mask.astype(F32)
        st_s[...] = jnp.zeros_like(st_s)

        def chunk_body(step, carry, d=d, mask=mask, tri=tri):
            if d == 0:
                c = step
            else:
                c = jnp.where(step < ncc, ncc - 1 - step, nc + ncc - 1 - step)
            r0 = pl.multiple_of(c * ck, ck)
            dtc = dt_s[pl.ds(r0, ck), :]
            cs = _dot(tri, dtc * a, HIGHEST)
            cs_t = cs.T
            bc = b_ref[0, pl.ds(r0, ck), :]
            cb = c_ref[0, pl.ds(r0, ck), :].astype(BF16)
            gmat = _dot_nt(cb, bc.astype(BF16))
            bt = bc.T.astype(BF16)
            for j in range(hpg):
                ln = d * SSD_HEADS + j
                col = cs[:, ln:ln + 1]
                row = cs_t[ln:ln + 1, :]
                lm = jnp.exp(jnp.where(mask, col - row, NEG))
                xh = x_ref[0, pl.ds(r0, ck), j * SSD_HEAD_DIM:(j + 1) * SSD_HEAD_DIM]
                xd = xh * dtc[:, ln:ln + 1]
                st = st_s[j]
                y = _dot((gmat * lm).astype(BF16), xd.astype(BF16))
                y = y + _dot(cb, st.astype(BF16)) * jnp.exp(col)
                tot = col[ck - 1:ck, :] if d == 0 else col[0:1, :]
                dec = jnp.exp(tot - col)
                st_s[j] = jnp.exp(tot) * st + _dot(bt, (xd * dec).astype(BF16))
                y = y + dsk[:, ln:ln + 1] * xh
                osl = (0, pl.ds(r0, ck), slice(j * SSD_HEAD_DIM, (j + 1) * SSD_HEAD_DIM))
                if d == 0:
                    o_ref[osl] = y
                else:
                    o_ref[osl] = o_ref[osl] + y
            return carry

        lax.fori_loop(0, nc, chunk_body, 0)


def _ssd(xbc, p, par, lc):
    bsz, s, _ = xbc.shape
    gw = SSD_WIDTH // SSD_GROUPS
    nb = SSD_WIDTH // SSD_STATE
    return pl.pallas_call(
        functools.partial(_ssd_kernel, lc=lc),
        grid=(bsz, SSD_GROUPS),
        in_specs=[
            pl.BlockSpec((1, s, gw), lambda b, g: (b, 0, g)),
            pl.BlockSpec((1, s, SSD_STATE), lambda b, g: (b, 0, nb + g)),
            pl.BlockSpec((1, s, SSD_STATE), lambda b, g: (b, 0, nb + SSD_GROUPS + g)),
            pl.BlockSpec((1, s, LANE), lambda b, g: (b, 0, COL_DT // LANE)),
            pl.BlockSpec((8, LANE), lambda b, g: (0, 0)),
        ],
        out_specs=pl.BlockSpec((1, s, gw), lambda b, g: (b, 0, g)),
        out_shape=jax.ShapeDtypeStruct((bsz, s, SSD_WIDTH), F32),
        scratch_shapes=[
            pltpu.VMEM((s, LANE), F32),
            pltpu.VMEM((SSD_HEADS // SSD_GROUPS, SSD_STATE, SSD_HEAD_DIM), F32),
        ],
        name="ssd_scan",
    )(xbc, xbc, xbc, p, par)


def _merge_kernel(att_ref, y_ref, z_ref, x_ref, g1_ref, sh2_ref, sc2_ref, ba_ref, bs_ref,
                  w_ref, lg_ref, lb_ref, x1_ref, h2_ref):
    att = _rms_norm(att_ref[0], ba_ref[...])
    ssm = _rms_norm(y_ref[0] * _silu(z_ref[0]), bs_ref[...])
    m = (_dot(att.astype(BF16), w_ref[:ATT_WIDTH, :])
         + _dot(ssm.astype(BF16), w_ref[ATT_WIDTH:, :]))
    xn = _layer_norm(DEEPNORM_ALPHA * x_ref[0] + g1_ref[0] * m, lg_ref[...], lb_ref[...])
    x1_ref[0] = xn
    h2_ref[0] = xn * (1.0 + sc2_ref[0]) + sh2_ref[0]


def _merge(att, y, p, x, mod, beta_a, beta_s, w_out, ln_g, ln_b, lc, ctx_row):
    bsz, s, d = x.shape
    tm = 256
    nct = lc // tm

    def mod_spec(k):
        return pl.BlockSpec((1, 1, d), lambda b, i: (_mod_row(b, i, nct, ctx_row), 0, k))

    def vec(n):
        return pl.BlockSpec((1, n), lambda b, i: (0, 0))

    half = pl.BlockSpec((1, tm, ATT_WIDTH), lambda b, i: (b, i, 0))
    full = pl.BlockSpec((1, tm, d), lambda b, i: (b, i, 0))
    return pl.pallas_call(
        _merge_kernel,
        grid=(bsz, s // tm),
        in_specs=[
            half, half,
            pl.BlockSpec((1, tm, SSD_WIDTH), lambda b, i: (b, i, COL_Z // SSD_WIDTH)),
            full, mod_spec(2), mod_spec(3), mod_spec(4),
            vec(ATT_WIDTH), vec(SSD_WIDTH),
            pl.BlockSpec((d, d), lambda b, i: (0, 0)),
            vec(d), vec(d),
        ],
        out_specs=[full, full],
        out_shape=[jax.ShapeDtypeStruct((bsz, s, d), F32)] * 2,
        name="merge_out_proj",
    )(att, y, p, x, mod, mod, mod, beta_a.reshape(1, -1), beta_s.reshape(1, -1), w_out,
      ln_g.reshape(1, d), ln_b.reshape(1, d))


def _topk_cols(s, k):
    n = s.shape[0]
    iota = lax.broadcasted_iota(jnp.int32, s.shape, 0)
    vals, idxs = [], []
    for _ in range(k):
        m = s.max(axis=0, keepdims=True)
        am = jnp.where(s == m, iota, n).min(axis=0, keepdims=True)
        vals.append(m)
        idxs.append(am)
        s = jnp.where(iota == am, -jnp.inf, s)
    return jnp.concatenate(vals, axis=0), jnp.concatenate(idxs, axis=0)


ROUTE_HEADS = 8


def _route_kernel(h_ref, wq_ref, keys_ref, idx_ref, gate_ref, q_s):
    hg = pl.program_id(1)
    half = PEER_QDIM // 2

    @pl.when(hg == 0)
    def _():
        q = _dot(h_ref[...].astype(BF16), wq_ref[...])
        for c in range(2 * PEER_HEADS):
            q_s[c] = q[:, c * half:(c + 1) * half].astype(BF16)

    for hh in range(ROUTE_HEADS):
        _route_head(hg * ROUTE_HEADS + hh, hh, keys_ref, idx_ref, gate_ref, q_s)


def _route_head(hd, hh, keys_ref, idx_ref, gate_ref, q_s):
    tops, topi = [], []
    for a in range(2):
        kk = keys_ref[hh, a].astype(BF16)
        v, i = _topk_cols(_dot_nt(kk, q_s[2 * hd + a]), PEER_TOPK)
        tops.append(v)
        topi.append(i)
    nj = [PEER_TOPK // (i + 1) for i in range(PEER_TOPK)]
    pad = -sum(nj) % 8
    tm = tops[0].shape[1]
    cand = jnp.concatenate([tops[0][i:i + 1] + tops[1][:nj[i]] for i in range(PEER_TOPK)]
                           + [jnp.full((pad, tm), -jnp.inf, F32)], axis=0)
    cidx = jnp.concatenate([topi[0][i:i + 1] * PEER_NKEYS + topi[1][:nj[i]] for i in range(PEER_TOPK)]
                           + [jnp.full((pad, tm), -1, jnp.int32)], axis=0)
    iota = lax.broadcasted_iota(jnp.int32, cand.shape, 0)
    best, bidx = [], []
    for _ in range(PEER_TOPK):
        m = cand.max(axis=0, keepdims=True)
        pos = jnp.where(cand == m, iota, cand.shape[0]).min(axis=0, keepdims=True)
        sel = iota == pos
        best.append(m)
        bidx.append(jnp.where(sel, cidx, -1).max(axis=0, keepdims=True))
        cand = jnp.where(sel, -jnp.inf, cand)
    bs = jnp.concatenate(best, axis=0)
    e = jnp.exp(bs - bs[0:1])
    gate_ref[hh] = e / e.sum(axis=0, keepdims=True)
    idx_ref[hh] = jnp.concatenate(bidx, axis=0)


def _route(h2, wq, keys):
    t, d = h2.shape
    tm = 128
    half = PEER_QDIM // 2
    out = jax.ShapeDtypeStruct((PEER_HEADS, PEER_TOPK, t), jnp.int32)
    return pl.pallas_call(
        _route_kernel,
        grid=(t // tm, PEER_HEADS // ROUTE_HEADS),
        in_specs=[
            pl.BlockSpec((tm, d), lambda i, h: (i, 0)),
            pl.BlockSpec((d, PEER_HEADS * PEER_QDIM), lambda i, h: (0, 0)),
            pl.BlockSpec((ROUTE_HEADS, 2, PEER_NKEYS, half), lambda i, h: (h, 0, 0, 0)),
        ],
        out_specs=[pl.BlockSpec((ROUTE_HEADS, PEER_TOPK, tm), lambda i, h: (h, 0, i))] * 2,
        out_shape=[out, jax.ShapeDtypeStruct(out.shape, F32)],
        scratch_shapes=[pltpu.VMEM((2 * PEER_HEADS, tm, half), BF16)],
        name="peer_route",
    )(h2, wq, keys)


PEER_TOKENS = 128
PEER_SLOTS = 8
PEER_PRIORITIES = (0, 1)


def _gelu_tanh(x):
    return 0.5 * x * (1.0 + jnp.tanh(math.sqrt(2.0 / math.pi) * (x + 0.044715 * (x * x * x))))


def _pack_kernel(u_ref, v_ref, o_ref):
    ub = lax.bitcast_convert_type(u_ref[...].astype(BF16).astype(F32), jnp.uint32)
    vb = lax.bitcast_convert_type(v_ref[...].astype(BF16).astype(F32), jnp.uint32)
    o_ref[...] = (ub >> 16) | vb


def _pack_tables(u_tab, v_tab):
    r, d = u_tab.shape
    tr = 256
    blk = pl.BlockSpec((tr, d), lambda i: (i, 0))
    return pl.pallas_call(
        _pack_kernel,
        grid=(r // tr,),
        in_specs=[blk, blk],
        out_specs=blk,
        out_shape=jax.ShapeDtypeStruct((r, d), jnp.uint32),
        name="peer_pack",
    )(u_tab, v_tab).reshape(r, 1, d)


def _peer_kernel(idx_hbm, gate_ref, h_ref, x1_ref, g2_ref, lg_ref, lb_ref, tab_hbm, o_ref,
                 idx_s, buf, f_s, isem, sem):
    i = pl.program_id(0)
    n = pl.num_programs(0)
    tt = PEER_TOKENS
    nsel = PEER_SEL
    ns = PEER_SLOTS
    ahead = ns - 1
    cur = i % 2

    def idx_copy(tile, half):
        return pltpu.make_async_copy(idx_hbm.at[pl.ds(tile * tt, tt), :], idx_s.at[half],
                                     isem.at[half])

    def issue(half, t, slot):
        for e in range(nsel):
            row = idx_s[half, t, e]
            pltpu.make_async_copy(tab_hbm.at[row], buf.at[slot, pl.ds(e, 1), :],
                                  sem.at[slot]).start(priority=PEER_PRIORITIES[e % len(PEER_PRIORITIES)])

    def wait(slot):
        pltpu.make_async_copy(buf.at[slot], buf.at[slot], sem.at[slot]).wait()

    sub = 8
    groups = nsel // sub
    chunks = D_MODEL // LANE
    lane = lax.broadcasted_iota(jnp.int32, (sub, nsel), 1)
    subl = lax.broadcasted_iota(jnp.int32, (sub, nsel), 0)

    def consume(t, slot):
        wait(slot)
        hrow = h_ref[pl.ds(t, 1), :]
        grow = jnp.broadcast_to(gate_ref[pl.ds(t, 1), :], (sub, nsel))
        ws = []
        for g in range(groups):
            words = buf[slot, g * sub:(g + 1) * sub, :]
            prod = lax.bitcast_convert_type(words << 16, F32) * hrow
            acc = prod[:, :LANE]
            for c in range(1, chunks):
                acc = acc + prod[:, c * LANE:(c + 1) * LANE]
            act = jnp.sum(acc, axis=1, keepdims=True)
            gcol = jnp.sum(jnp.where(lane == subl + g * sub, grow, 0.0), axis=1, keepdims=True)
            ws.append(gcol * _gelu_tanh(act))
        f = None
        for g in range(groups):
            words = buf[slot, g * sub:(g + 1) * sub, :]
            term = ws[g] * lax.bitcast_convert_type(words & jnp.uint32(0xFFFF0000), F32)
            f = term if f is None else f + term
        f_s[pl.ds(t, 1), :] = jnp.sum(f, axis=0, keepdims=True)

    @pl.when(i == 0)
    def _():
        first = idx_copy(0, 0)
        first.start()
        first.wait()
        for t in range(ahead):
            issue(0, t, t)

    @pl.when(i + 1 < n)
    def _():
        idx_copy(i + 1, 1 - cur).start()

    def block(b, carry):
        for k in range(ns):
            t = b * ns + k
            consume(t, k)
            issue(cur, t + ahead, (k + ahead) % ns)
        return carry

    lax.fori_loop(0, tt // ns - 1, block, 0)

    @pl.when(i + 1 < n)
    def _():
        idx_copy(i + 1, 1 - cur).wait()

    for k in range(ns):
        t = tt - ns + k
        consume(t, k)
        if k == 0:
            issue(cur, t + ahead, (k + ahead) % ns)
        else:
            @pl.when(i + 1 < n)
            def _(k=k):
                issue(1 - cur, k - 1, (k + ahead) % ns)

    o_ref[...] = _layer_norm(DEEPNORM_ALPHA * x1_ref[...] + g2_ref[0] * f_s[...],
                             lg_ref[...], lb_ref[...])


def _peer(idx, gate, h2, x1, mod, ln_g, ln_b, tab, s, lc, ctx_row):
    t, d = h2.shape
    tt = PEER_TOKENS
    tiles_per_b = s // tt
    nct = lc // tt

    def mod_row(i):
        return jnp.where(i % tiles_per_b < nct, ctx_row, i // tiles_per_b)

    tok = pl.BlockSpec((tt, d), lambda i: (i, 0))
    vec = pl.BlockSpec((1, d), lambda i: (0, 0))
    return pl.pallas_call(
        _peer_kernel,
        grid=(t // tt,),
        in_specs=[
            pl.BlockSpec(memory_space=pl.ANY),
            pl.BlockSpec((tt, PEER_SEL), lambda i: (i, 0)),
            tok, tok,
            pl.BlockSpec((1, 1, d), lambda i: (mod_row(i), 0, 5)),
            vec, vec,
            pl.BlockSpec(memory_space=pl.ANY),
        ],
        out_specs=tok,
        out_shape=jax.ShapeDtypeStruct((t, d), F32),
        scratch_shapes=[
            pltpu.SMEM((2, tt, PEER_SEL), jnp.int32),
            pltpu.VMEM((PEER_SLOTS, PEER_SEL, d), jnp.uint32),
            pltpu.VMEM((tt, d), F32),
            pltpu.SemaphoreType.DMA((2,)),
            pltpu.SemaphoreType.DMA((PEER_SLOTS,)),
        ],
        compiler_params=pltpu.CompilerParams(dimension_semantics=("arbitrary",)),
        name="peer_experts",
    )(idx, gate, h2, x1, mod, ln_g.reshape(1, d), ln_b.reshape(1, d), tab)


def kernel(x, c, ctx, c_ctx, ada_w, ada_b, w_in, conv_w, conv_b, a_log, dt_bias, d_skip, rpb,
           beta_attn, beta_ssm, w_out, ln1_g, ln1_b, peer_wq, peer_keys, peer_u, peer_v,
           ln2_g, ln2_b):
    bsz, l, d = x.shape
    lc = ctx.shape[1]
    s = lc + l
    depth = w_in.shape[0]
    rows = l // GRID_W

    n_rows = -(-(bsz + 1) // 8) * 8
    cond = jnp.concatenate([c, c_ctx[None, :], jnp.zeros((n_rows - bsz - 1, d), F32)], axis=0)
    mods = _ada(cond, ada_w, ada_b).reshape(depth, n_rows, 1, N_MOD * d)
    cos, s1, s2 = _rope_tables(l)
    n_exp = peer_u.shape[1]
    tab = _pack_tables(peer_u.reshape(depth * n_exp, d), peer_v.reshape(depth * n_exp, d))

    xa = jnp.concatenate([ctx, x], axis=1)
    for layer in range(depth):
        mod = mods[layer]
        w_in_p = jnp.pad(w_in[layer], ((0, 0), (0, PROJ_PAD - PROJ_WIDTH))).astype(BF16)
        p = _proj(xa, mod, w_in_p, lc, bsz)
        att = _attention(p, _bias_strips(rpb[layer]), cos, s1, s2, lc)
        xbc = _conv(p, conv_w[layer], conv_b[layer], lc)
        par = jnp.zeros((8, LANE), F32)
        par = par.at[0, :2 * SSD_HEADS].set(dt_bias[layer].reshape(-1))
        par = par.at[1, :2 * SSD_HEADS].set(a_log[layer].reshape(-1))
        par = par.at[2, :2 * SSD_HEADS].set(d_skip[layer].reshape(-1))
        y = _ssd(xbc, p, par, lc)
        x1, h2 = _merge(att, y, p, xa, mod, beta_attn[layer], beta_ssm[layer],
                        w_out[layer].astype(BF16), ln1_g[layer], ln1_b[layer], lc, bsz)
        last = layer == depth - 1
        s_t, lc_t = (l, 0) if last else (s, lc)
        if last:
            h2, x1 = h2[:, lc:, :], x1[:, lc:, :]
        h2f = h2.reshape(bsz * s_t, d)
        idx, gate = _route(h2f, peer_wq[layer].astype(BF16), peer_keys[layer])
        idx = idx.transpose(2, 0, 1).reshape(bsz * s_t, PEER_SEL) + layer * n_exp
        gate = gate.transpose(2, 0, 1).reshape(bsz * s_t, PEER_SEL)
        xa = _peer(idx, gate, h2f, x1.reshape(bsz * s_t, d), mod, ln2_g[layer], ln2_b[layer],
                   tab, s_t, lc_t, bsz).reshape(bsz, s_t, d)
    return xa
```
